```python
import math
import jax
import jax.numpy as jnp
from jax import lax
import numpy as np

D_MODEL = 1024
BATCH = 32
SEQ = 256
DEPTH = 4
DEC_BATCH = 2
DEC_SEQ = 2048
PAST_LEN = 256

GRID_W = 64

N_MIXERS = 3
N_HYENA_LAYERS = (DEPTH + 2) // 3
N_MLSTM_LAYERS = (DEPTH + 1) // 3
N_CONV_LAYERS = DEPTH // 3
N_DENSE_LAYERS = (DEPTH + 1) // 2
N_MOE_LAYERS = DEPTH // 2

HY_ORDER = 2
HY_SHORT = 3
HY_BANDS = 16
HY_EMB = 1 + 2 * HY_BANDS
HY_HIDDEN = 64
HY_SLOW_DECAY = -math.log(1e-2) / 1.5
HY_FAST_DECAY = -math.log(1e-2) / 0.3
HY_SHIFT = 0.05

ML_INNER = 2 * D_MODEL
ML_HEADS = 8
ML_DH = ML_INNER // ML_HEADS
ML_CONV = 5
ML_CHUNK = 64

CV_K = 31

D_FF = 2816
N_EXPERTS = 8
TOP_K = 2
D_FF_EXPERT = 3584

EPS = 1e-6

kernel_name = 'hybrid_hyena_mlstm_conformer_prefix_step'


def rmsnorm(x, g):
    xf = x.astype(jnp.float32)
    y = xf * lax.rsqrt(jnp.mean(xf * xf, axis=-1, keepdims=True) + EPS)
    return (y * g.astype(jnp.float32)).astype(x.dtype)


def layernorm(x, g, b):
    xf = x.astype(jnp.float32)
    mu = jnp.mean(xf, axis=-1, keepdims=True)
    var = jnp.mean(jnp.square(xf - mu), axis=-1, keepdims=True)
    y = (xf - mu) * lax.rsqrt(var + EPS) * g.astype(jnp.float32) + b.astype(jnp.float32)
    return y.astype(x.dtype)


def dwconv(x, w):
    return lax.conv_general_dilated(
        x, w[:, None, :].astype(x.dtype), window_strides=(1,), padding='SAME',
        dimension_numbers=('NWC', 'WIO', 'NWC'), feature_group_count=x.shape[-1])


def hyena_filters(L, w1, b1, w2, b2, w3):
    f32 = jnp.float32
    t = jnp.arange(L, dtype=f32) / L
    freqs = jnp.arange(1, HY_BANDS + 1, dtype=f32)
    ang = 2.0 * math.pi * t[:, None] * freqs[None, :]
    feats = jnp.concatenate([t[:, None], jnp.cos(ang), jnp.sin(ang)], axis=-1)
    h = jnp.sin(feats @ w1.astype(f32) + b1.astype(f32))
    h = jnp.sin(h @ w2.astype(f32) + b2.astype(f32))
    h = (h @ w3.astype(f32)).reshape(L, 2, HY_ORDER, D_MODEL)
    deltas = jnp.linspace(HY_SLOW_DECAY, HY_FAST_DECAY, D_MODEL, dtype=f32)
    window = jnp.exp(-t[:, None] * deltas[None, :]) + HY_SHIFT
    h = h * window[:, None, None, :]
    h = h * lax.rsqrt(jnp.sum(h * h, axis=(0, 1), keepdims=True) + EPS)
    hf, hb = h[:, 0], h[:, 1]
    g = jnp.concatenate([hf.at[0].add(hb[0]), jnp.zeros_like(hf[:1]), hb[1:][::-1]], axis=0)
    return jnp.fft.rfft(g, axis=0)


def long_conv(z, G, bias):
    L = z.shape[1]
    zf = z.astype(jnp.float32)
    y = jnp.fft.irfft(jnp.fft.rfft(zf, n=2 * L, axis=1) * G[None], n=2 * L, axis=1)[:, :L]
    return (y + zf * bias.astype(jnp.float32)).astype(z.dtype)


def hyena_mixer(u, w_in, short, w1, b1, w2, b2, w3, filt_bias, w_out):
    L = u.shape[1]
    v, x1, x2 = jnp.split(dwconv(u @ w_in, short), 3, axis=-1)
    G = hyena_filters(L, w1, b1, w2, b2, w3)
    z = x1 * long_conv(v, G[:, 0], filt_bias[0])
    z = x2 * long_conv(z, G[:, 1], filt_bias[1])
    return z @ w_out


def mlstm_chunkwise(q, k, v, log_i, log_f, C0, n0, m0):
    B, H, L, DH = q.shape
    nc = L // ML_CHUNK

    def to_chunks(a):
        a = a.reshape(a.shape[:2] + (nc, ML_CHUNK) + a.shape[3:])
        return jnp.moveaxis(a, 2, 0)

    xs = tuple(to_chunks(a) for a in (q, k, v, log_i, log_f))
    causal = jnp.tril(jnp.ones((ML_CHUNK, ML_CHUNK), dtype=bool))

    def step(carry, chunk):
        C, n, m = carry
        qc, kc, vc, li, lf = chunk
        b = jnp.cumsum(lf, axis=-1)
        a = b + m[..., None]
        dmat = jnp.where(causal, b[..., :, None] - b[..., None, :] + li[..., None, :], -jnp.inf)
        mt = jnp.maximum(a, jnp.max(dmat, axis=-1))
        w_inter = jnp.exp(a - mt)
        s = jnp.einsum('bhtd,bhsd->bhts', qc, kc) * jnp.exp(dmat - mt[..., None])
        num = w_inter[..., None] * jnp.einsum('bhtd,bhde->bhte', qc, C) + jnp.einsum('bhts,bhse->bhte', s, vc)
        den = w_inter * jnp.einsum('bhtd,bhd->bht', qc, n) + jnp.sum(s, axis=-1)
        h = num / jnp.maximum(jnp.abs(den), jnp.exp(-mt))[..., None]
        b_last = b[..., -1]
        d_last = b_last[..., None] - b + li
        m_new = jnp.maximum(b_last + m, jnp.max(d_last, axis=-1))
        w_old = jnp.exp(b_last + m - m_new)
        kw = kc * jnp.exp(d_last - m_new[..., None])[..., None]
        C_new = w_old[..., None, None] * C + jnp.einsum('bhsd,bhse->bhde', kw, vc)
        n_new = w_old[..., None] * n + jnp.sum(kw, axis=2)
        return (C_new, n_new, m_new), h

    (C, n, m), hs = lax.scan(step, (C0, n0, m0), xs)
    h = jnp.moveaxis(hs, 0, 2).reshape(B, H, L, DH)
    return h, C, n, m


def mlstm_mixer(u, C0, n0, m0, w_up, conv_w, w_q, w_k, w_v, w_gate, b_gate, norm_g, skip, w_down):
    f32 = jnp.float32
    B, L, _ = u.shape
    xm, z = jnp.split(u @ w_up, 2, axis=-1)
    xc = jax.nn.silu(dwconv(xm, conv_w))
    xch = xc.reshape(B, L, ML_HEADS, ML_DH)
    q = jnp.einsum('blhd,hde->blhe', xch, w_q) * (ML_DH ** -0.5)
    k = jnp.einsum('blhd,hde->blhe', xch, w_k)
    v = jnp.einsum('blhd,hde->blhe', xm.reshape(B, L, ML_HEADS, ML_DH), w_v)
    qkv = jnp.concatenate([a.reshape(B, L, ML_INNER) for a in (q, k, v)], axis=-1).astype(f32)
    gates = jnp.einsum('blc,rcg->rblg', qkv, w_gate.astype(f32)) + b_gate.astype(f32)[:, None, None, :]
    qh, kh, vh = (jnp.transpose(a, (0, 2, 1, 3)).astype(f32) for a in (q, k, v))
    hs, Cs, ns, ms = [], [], [], []
    for r in range(2):
        li = jnp.transpose(gates[r, ..., :ML_HEADS], (0, 2, 1))
        lf = jax.nn.log_sigmoid(jnp.transpose(gates[r, ..., ML_HEADS:], (0, 2, 1)))
        seqs = (qh, kh, vh, li, lf)
        if r == 1:
            seqs = tuple(jnp.flip(a, axis=2) for a in seqs)
        h, C, n, m = mlstm_chunkwise(*seqs, C0[:, r].astype(f32), n0[:, r].astype(f32), m0[:, r].astype(f32))
        hs.append(jnp.flip(h, axis=2) if r == 1 else h)
        Cs.append(C)
        ns.append(n)
        ms.append(m)
    h = hs[0] + hs[1]
    mu = jnp.mean(h, axis=-1, keepdims=True)
    var = jnp.mean(jnp.square(h - mu), axis=-1, keepdims=True)
    h = (h - mu) * lax.rsqrt(var + EPS)
    h = jnp.transpose(h, (0, 2, 1, 3)).reshape(B, L, ML_INNER) * norm_g.astype(f32) + skip.astype(f32) * xc.astype(f32)
    y = (h.astype(u.dtype) * jax.nn.silu(z)) @ w_down
    return y, jnp.stack(Cs, axis=1), jnp.stack(ns, axis=1), jnp.stack(ms, axis=1)


def conformer_conv(u, w1, b1, dw, b_dw, ln_g, ln_b, w2, b2):
    a, gt = jnp.split(u @ w1 + b1, 2, axis=-1)
    y = a * jax.nn.sigmoid(gt)
    y = dwconv(y, dw) + b_dw
    y = jax.nn.silu(layernorm(y, ln_g, ln_b))
    return y @ w2 + b2


def swiglu(u, wg, wu, wd):
    return (jax.nn.silu(u @ wg) * (u @ wu)) @ wd


def moe_swiglu(u, router, wg, wu, wd):
    B, L, D = u.shape
    t = u.reshape(B * L, D)
    logits = (t @ router).astype(jnp.float32)
    top_v, top_i = lax.top_k(logits, TOP_K)
    probs = jax.nn.softmax(top_v, axis=-1)
    combine = jnp.sum(jax.nn.one_hot(top_i, N_EXPERTS, dtype=jnp.float32) * probs[..., None], axis=1).astype(t.dtype)
    out = jnp.zeros_like(t)
    for e in range(N_EXPERTS):
        out = out + combine[:, e:e + 1] * swiglu(t, wg[e], wu[e], wd[e])
    return out.reshape(B, L, D)


def setup_inputs(seed: int = 0) -> dict:
    key = jax.random.key(seed)
    keys = jax.random.split(key, 64)
    counter = [0]

    def nrm(shape, scale):
        kk = keys[counter[0]]
        counter[0] += 1
        return jax.random.normal(kk, shape, jnp.float32) * scale

    D = D_MODEL
    H = ML_HEADS
    return {
        'x_prompt': nrm((BATCH, SEQ, D), 1.0),
        'x_sample': nrm((DEC_BATCH, DEC_SEQ, D), 1.0),
        'state_mlstm_C': nrm((DEC_BATCH, N_MLSTM_LAYERS, 2, H, ML_DH, ML_DH), 0.1),
        'state_mlstm_n': nrm((DEC_BATCH, N_MLSTM_LAYERS, 2, H, ML_DH), 0.1),
        'state_mlstm_m': nrm((DEC_BATCH, N_MLSTM_LAYERS, 2, H), 1.0),
        'c': nrm((DEC_BATCH, D), 1.0),
        'c_ctx': nrm((D,), 1.0),
        'w_ada': nrm((DEPTH, D, 6 * D), 0.5 * D ** -0.5),
        'b_ada': nrm((DEPTH, 6 * D), 0.02),
        'norm_mix_g': 1.0 + nrm((DEPTH, D), 0.05),
        'norm_ffn_g': 1.0 + nrm((DEPTH, D), 0.05),
        'hy_w_in': nrm((N_HYENA_LAYERS, D, 3 * D), D ** -0.5),
        'hy_short': nrm((N_HYENA_LAYERS, HY_SHORT, 3 * D), HY_SHORT ** -0.5),
        'hy_filt_w1': nrm((N_HYENA_LAYERS, HY_EMB, HY_HIDDEN), 1.0),
        'hy_filt_b1': nrm((N_HYENA_LAYERS, HY_HIDDEN), 0.1),
        'hy_filt_w2': nrm((N_HYENA_LAYERS, HY_HIDDEN, HY_HIDDEN), HY_HIDDEN ** -0.5),
        'hy_filt_b2': nrm((N_HYENA_LAYERS, HY_HIDDEN), 0.1),
        'hy_filt_w3': nrm((N_HYENA_LAYERS, HY_HIDDEN, 2 * HY_ORDER * D), HY_HIDDEN ** -0.5),
        'hy_filt_bias': nrm((N_HYENA_LAYERS, HY_ORDER, D), 0.1),
        'hy_w_out': nrm((N_HYENA_LAYERS, D, D), D ** -0.5),
        'ml_w_up': nrm((N_MLSTM_LAYERS, D, 2 * ML_INNER), D ** -0.5),
        'ml_conv': nrm((N_MLSTM_LAYERS, ML_CONV, ML_INNER), ML_CONV ** -0.5),
        'ml_w_q': nrm((N_MLSTM_LAYERS, H, ML_DH, ML_DH), ML_DH ** -0.5),
        'ml_w_k': nrm((N_MLSTM_LAYERS, H, ML_DH, ML_DH), ML_DH ** -0.5),
        'ml_w_v': nrm((N_MLSTM_LAYERS, H, ML_DH, ML_DH), ML_DH ** -0.5),
        'ml_w_gate': nrm((N_MLSTM_LAYERS, 2, 3 * ML_INNER, 2 * H), (3 * ML_INNER) ** -0.5),
        'ml_b_gate': jnp.concatenate([nrm((N_MLSTM_LAYERS, 2, H), 0.1),
                                      jnp.linspace(3.0, 6.0, H, dtype=jnp.float32) + nrm((N_MLSTM_LAYERS, 2, H), 0.1)], axis=-1),
        'ml_norm_g': 1.0 + nrm((N_MLSTM_LAYERS, ML_INNER), 0.05),
        'ml_skip': 1.0 + nrm((N_MLSTM_LAYERS, ML_INNER), 0.05),
        'ml_w_down': nrm((N_MLSTM_LAYERS, ML_INNER, D), ML_INNER ** -0.5),
        'cv_w1': nrm((N_CONV_LAYERS, D, 2 * D), D ** -0.5),
        'cv_b1': nrm((N_CONV_LAYERS, 2 * D), 0.02),
        'cv_dw': nrm((N_CONV_LAYERS, CV_K, D), CV_K ** -0.5),
        'cv_b_dw': nrm((N_CONV_LAYERS, D), 0.02),
        'cv_ln_g': 1.0 + nrm((N_CONV_LAYERS, D), 0.05),
        'cv_ln_b': nrm((N_CONV_LAYERS, D), 0.02),
        'cv_w2': nrm((N_CONV_LAYERS, D, D), D ** -0.5),
        'cv_b2': nrm((N_CONV_LAYERS, D), 0.02),
        'ff_w_gate': nrm((N_DENSE_LAYERS, D, D_FF), D ** -0.5),
        'ff_w_up': nrm((N_DENSE_LAYERS, D, D_FF), D ** -0.5),
        'ff_w_down': nrm((N_DENSE_LAYERS, D_FF, D), D_FF ** -0.5),
        'moe_router': nrm((N_MOE_LAYERS, D, N_EXPERTS), D ** -0.5),
        'moe_w_gate': nrm((N_MOE_LAYERS, N_EXPERTS, D, D_FF_EXPERT), D ** -0.5),
        'moe_w_up': nrm((N_MOE_LAYERS, N_EXPERTS, D, D_FF_EXPERT), D ** -0.5),
        'moe_w_down': nrm((N_MOE_LAYERS, N_EXPERTS, D_FF_EXPERT, D), D_FF_EXPERT ** -0.5),
        'final_norm_g': 1.0 + nrm((D,), 0.05),
    }


def reference(x_prompt, x_sample, state_mlstm_C, state_mlstm_n, state_mlstm_m, c, c_ctx,
              w_ada, b_ada, norm_mix_g, norm_ffn_g,
              hy_w_in, hy_short, hy_filt_w1, hy_filt_b1, hy_filt_w2, hy_filt_b2, hy_filt_w3, hy_filt_bias, hy_w_out,
              ml_w_up, ml_conv, ml_w_q, ml_w_k, ml_w_v, ml_w_gate, ml_b_gate, ml_norm_g, ml_skip, ml_w_down,
              cv_w1, cv_b1, cv_dw, cv_b_dw, cv_ln_g, cv_ln_b, cv_w2, cv_b2,
              ff_w_gate, ff_w_up, ff_w_down,
              moe_router, moe_w_gate, moe_w_up, moe_w_down,
              final_norm_g):

    def trunk(x, cond, C0, n0, m0):
        s = jax.nn.silu(cond.astype(jnp.float32))
        Cs, ns, ms = [], [], []
        for i in range(DEPTH):
            mod = (s @ w_ada[i].astype(jnp.float32) + b_ada[i].astype(jnp.float32)).astype(x.dtype)
            sh1, sc1, g1, sh2, sc2, g2 = jnp.split(mod[:, None, :], 6, axis=-1)
            u = rmsnorm(x, norm_mix_g[i]) * (1 + sc1) + sh1
            kind, j = i % N_MIXERS, i // N_MIXERS
            if kind == 0:
                y = hyena_mixer(u, hy_w_in[j], hy_short[j], hy_filt_w1[j], hy_filt_b1[j], hy_filt_w2[j],
                                hy_filt_b2[j], hy_filt_w3[j], hy_filt_bias[j], hy_w_out[j])
            elif kind == 1:
                y, C, n, m = mlstm_mixer(u, C0[:, j], n0[:, j], m0[:, j], ml_w_up[j], ml_conv[j], ml_w_q[j],
                                         ml_w_k[j], ml_w_v[j], ml_w_gate[j], ml_b_gate[j], ml_norm_g[j],
                                         ml_skip[j], ml_w_down[j])
                Cs.append(C)
                ns.append(n)
                ms.append(m)
            else:
                y = conformer_conv(u, cv_w1[j], cv_b1[j], cv_dw[j], cv_b_dw[j], cv_ln_g[j], cv_ln_b[j],
                                   cv_w2[j], cv_b2[j])
            x = x + g1 * y
            u = rmsnorm(x, norm_ffn_g[i]) * (1 + sc2) + sh2
            jf = i // 2
            if i % 2 == 0:
                y = swiglu(u, ff_w_gate[jf], ff_w_up[jf], ff_w_down[jf])
            else:
                y = moe_swiglu(u, moe_router[jf], moe_w_gate[jf], moe_w_up[jf], moe_w_down[jf])
            x = x + g2 * y
        return rmsnorm(x, final_norm_g), jnp.stack(Cs, axis=1), jnp.stack(ns, axis=1), jnp.stack(ms, axis=1)

    bp = x_prompt.shape[0]
    zC = jnp.zeros((bp, N_MLSTM_LAYERS, 2, ML_HEADS, ML_DH, ML_DH), jnp.float32)
    zn = jnp.zeros((bp, N_MLSTM_LAYERS, 2, ML_HEADS, ML_DH), jnp.float32)
    zm = jnp.zeros((bp, N_MLSTM_LAYERS, 2, ML_HEADS), jnp.float32)
    y_prompt, new_mlstm_C, new_mlstm_n, new_mlstm_m = trunk(x_prompt, c_ctx[None, :], zC, zn, zm)

    y_sample, _, _, _ = trunk(x_sample, c, state_mlstm_C, state_mlstm_n, state_mlstm_m)

    return (y_prompt, y_sample, new_mlstm_C, new_mlstm_n, new_mlstm_m)
```

```python
import collections
import functools
import math

import jax
import jax.numpy as jnp
from jax import lax
from jax.experimental import pallas as pl
from jax.experimental.pallas import tpu as pltpu

F32 = jnp.float32
BF16 = jnp.bfloat16
HIGHEST = lax.Precision.HIGHEST

D_MODEL = 1024
HY_BANDS = 16
HY_SLOW_DECAY = -math.log(1e-2) / 1.5
HY_FAST_DECAY = -math.log(1e-2) / 0.3
HY_SHIFT = 0.05
ML_HEADS = 8
ML_DH = 256
N_EXPERTS = 8
EPS = 1e-6

LANES = 128
CHUNK = 256
CONV_HALO = 16
CONV_ROWS = 64
VMEM_CAP = 56 * 1024 * 1024

Geo = collections.namedtuple("Geo", "n_p l_p n_s l_s")


def _tp(geo):
    return geo.n_p * geo.l_p


def _tt(geo):
    return geo.n_p * geo.l_p + geo.n_s * geo.l_s


def _cond_of_block(geo, rows):
    tp = _tp(geo)

    def f(i):
        r0 = i * rows
        return jnp.where(r0 < tp, 0, 1 + (r0 - tp) // geo.l_s)

    return f


def _vmem_limit(*nbytes):
    est = 2 * sum(nbytes) + (8 << 20)
    return int(min(max(est, 32 << 20), VMEM_CAP))


def _nbytes(shape, dtype):
    n = 1
    for s in shape:
        if s is not None:
            n *= s
    return n * jnp.dtype(dtype).itemsize


def _silu(x):
    return x * (1.0 / (1.0 + jnp.exp(-x)))


def _sigmoid(x):
    return 1.0 / (1.0 + jnp.exp(-x))


def _mm_kernel(*refs, na, nb, ne, no, pairs, cast_b, epilogue, a_fn, inner_axis):
    a_refs = refs[:na]
    b_refs = refs[na:na + nb]
    e_refs = refs[na + nb:na + nb + ne]
    o_refs = refs[na + nb + ne:na + nb + ne + no]
    scratch = refs[na + nb + ne + no:]
    first = pl.program_id(inner_axis) == 0
    b_src = []
    si = 0
    for j in range(nb):
        if cast_b[j]:
            s_ref = scratch[si]
            si += 1

            @pl.when(first)
            def _(s_ref=s_ref, b_ref=b_refs[j]):
                s_ref[...] = b_ref[...].astype(BF16)

            b_src.append(s_ref)
        else:
            b_src.append(b_refs[j])
    a_vals = {}
    accs = []
    for (i, j) in pairs:
        if i not in a_vals:
            a = a_refs[i][...]
            if a_fn is not None:
                a = a_fn(a)
            a_vals[i] = a.astype(BF16)
        accs.append(jnp.dot(a_vals[i], b_src[j][...], preferred_element_type=F32))
    outs = epilogue(accs, [e[...] for e in e_refs])
    for o_ref, val in zip(o_refs, outs):
        o_ref[...] = val.astype(o_ref.dtype)


def _mm(a_list, b_list, pairs, *, grid, a_specs, b_specs, out_shapes, out_specs,
        epilogue, extras=(), extra_specs=(), a_fn=None):
    cast_b = tuple(b.dtype != BF16 for b in b_list)
    scratch = [pltpu.VMEM(tuple(s for s in spec.block_shape if s is not None), BF16)
               for b, spec, c in zip(b_list, b_specs, cast_b) if c]
    kern = functools.partial(
        _mm_kernel, na=len(a_list), nb=len(b_list), ne=len(extras), no=len(out_shapes),
        pairs=tuple(pairs), cast_b=cast_b, epilogue=epilogue, a_fn=a_fn, inner_axis=len(grid) - 1)
    sizes = [_nbytes(s.block_shape, a.dtype) for a, s in zip(a_list, a_specs)]
    sizes += [_nbytes(s.block_shape, b.dtype) for b, s in zip(b_list, b_specs)]
    sizes += [_nbytes(s.block_shape, e.dtype) for e, s in zip(extras, extra_specs)]
    sizes += [_nbytes(s.block_shape, o.dtype) for o, s in zip(out_shapes, out_specs)]
    sizes += [_nbytes(s.block_shape, BF16) // 2 for s, c in zip(b_specs, cast_b) if c]
    res = pl.pallas_call(
        kern, grid=grid,
        in_specs=list(a_specs) + list(b_specs) + list(extra_specs),
        out_specs=list(out_specs), out_shape=list(out_shapes),
        scratch_shapes=scratch,
        compiler_params=pltpu.CompilerParams(
            dimension_semantics=("arbitrary",) * len(grid), vmem_limit_bytes=_vmem_limit(*sizes)),
    )(*a_list, *b_list, *extras)
    return res


def _pick(n, prefs):
    for p in prefs:
        if n % p == 0:
            return p
    return n


def _row_tile(m, geo):
    base = math.gcd(m, geo.l_s, _tp(geo) or m) if geo is not None else m
    return _pick(base, (1024, 512, 256, 128, 8))


def _linear(a, w, *, out_dtype, epilogue=None, extras=(), extra_specs_fn=None, tm=None, tn=None,
            a_col=0, w_col=0, n_out=None, a_fn=None, k=None, geo=None):
    m = a.shape[0]
    k = k or w.shape[0]
    n_out = n_out or w.shape[1]
    tm = tm or _row_tile(m, geo)
    tn = tn or _pick(n_out, (512, 256, 128))
    grid = (n_out // tn, m // tm)
    jo = w_col * (n_out // tn)
    especs = extra_specs_fn(lambda j, i: i, lambda j, i: j, tm, tn) if extras else ()
    ep = epilogue or (lambda accs, ex: (accs[0],))
    (out,) = _mm(
        [a], [w], [(0, 0)], grid=grid,
        a_specs=[pl.BlockSpec((tm, k), lambda j, i: (i, a_col))],
        b_specs=[pl.BlockSpec((k, tn), lambda j, i: (0, jo + j))],
        out_shapes=[jax.ShapeDtypeStruct((m, n_out), out_dtype)],
        out_specs=[pl.BlockSpec((tm, tn), lambda j, i: (i, j))],
        epilogue=ep, extras=extras, extra_specs=especs, a_fn=a_fn)
    return out


def _residual_specs(geo):
    def fn(i_of, j_of, tm, tn):
        cond = _cond_of_block(geo, tm)
        return [pl.BlockSpec((tm, tn), lambda *g: (i_of(*g), j_of(*g))),
                pl.BlockSpec((None, 1, tn), lambda *g: (cond(i_of(*g)), 0, j_of(*g)))]
    return fn


def _residual_bias_specs(geo):
    base = _residual_specs(geo)

    def fn(i_of, j_of, tm, tn):
        return base(i_of, j_of, tm, tn) + [pl.BlockSpec((1, tn), lambda *g: (0, j_of(*g)))]
    return fn


def _ep_residual(accs, ex):
    return (ex[0] + ex[1] * accs[0],)


def _ep_residual_bias(accs, ex):
    return (ex[0] + ex[1] * (accs[0] + ex[2]),)


def _dual_linear(a, w1, w2, *, epilogue, out_dtype, n_out, w1_col=0, w2_col=0, extras=(),
                 extra_specs_fn=None, tm=None, tn=None, geo=None):
    m, k = a.shape
    tm = tm or _row_tile(m, geo)
    tn = tn or _pick(n_out, (512, 256, 128))
    nj = n_out // tn
    grid = (nj, m // tm)
    especs = extra_specs_fn(lambda j, i: i, lambda j, i: j, tm, tn) if extras else ()
    (out,) = _mm(
        [a], [w1, w2], [(0, 0), (0, 1)], grid=grid,
        a_specs=[pl.BlockSpec((tm, k), lambda j, i: (i, 0))],
        b_specs=[pl.BlockSpec((k, tn), lambda j, i: (0, w1_col * nj + j)),
                 pl.BlockSpec((k, tn), lambda j, i: (0, w2_col * nj + j))],
        out_shapes=[jax.ShapeDtypeStruct((m, n_out), out_dtype)],
        out_specs=[pl.BlockSpec((tm, tn), lambda j, i: (i, j))],
        epilogue=epilogue, extras=extras, extra_specs=especs)
    return out


def _rowwise_kernel(*refs, nin, fn):
    ins = [r[...] for r in refs[:nin]]
    outs = fn(*ins)
    for o_ref, val in zip(refs[nin:], outs):
        o_ref[...] = val.astype(o_ref.dtype)


def _rowwise(fn, geo, *, rows=(), params=(), conds=(), outs, tm, n_rows=None, row_block0=0):
    n_rows = n_rows or rows[0][0].shape[0]
    cond = _cond_of_block(geo, tm)
    in_specs, args, sizes = [], [], []
    for arr, w, cb in rows:
        in_specs.append(pl.BlockSpec((tm, w), lambda i, cb=cb: (i + row_block0, cb)))
        args.append(arr)
        sizes.append(_nbytes((tm, w), arr.dtype))
    for arr, w, cb in params:
        in_specs.append(pl.BlockSpec((1, w), lambda i, cb=cb: (0, cb)))
        args.append(arr)
    for arr in conds:
        w = arr.shape[-1]
        in_specs.append(pl.BlockSpec((None, 1, w), lambda i: (cond(i + row_block0), 0, 0)))
        args.append(arr)
    out_shapes = [jax.ShapeDtypeStruct((n_rows, w), dt) for w, dt in outs]
    out_specs = [pl.BlockSpec((tm, w), lambda i: (i, 0)) for w, dt in outs]
    sizes += [_nbytes((tm, w), dt) for w, dt in outs]
    return pl.pallas_call(
        functools.partial(_rowwise_kernel, nin=len(args), fn=fn),
        grid=(n_rows // tm,), in_specs=in_specs, out_specs=out_specs, out_shape=out_shapes,
        compiler_params=pltpu.CompilerParams(
            dimension_semantics=("arbitrary",), vmem_limit_bytes=_vmem_limit(*sizes, *sizes)),
    )(*args)


def _rms(x, g):
    return x * lax.rsqrt(jnp.mean(x * x, axis=-1, keepdims=True) + EPS) * g


def _norm_mod_fn(x, g, sc, sh):
    return (_rms(x, g) * (1.0 + sc) + sh,)


def _norm_mod(x, g, sc, sh, geo):
    (u,) = _rowwise(_norm_mod_fn, geo, rows=[(x, D_MODEL, 0)], params=[(g, D_MODEL, 0)],
                    conds=[sc, sh], outs=[(D_MODEL, BF16)], tm=512)
    return u


def _norm_mod_route_fn(x, g, router, sc, sh):
    u = _rms(x, g) * (1.0 + sc) + sh
    logits = jnp.dot(u, router, preferred_element_type=F32, precision=HIGHEST)
    lane = lax.broadcasted_iota(jnp.int32, logits.shape, 1)
    neg = jnp.float32(-jnp.inf)
    logits = jnp.where(lane < N_EXPERTS, logits, neg)
    v1 = jnp.max(logits, axis=-1, keepdims=True)
    i1 = jnp.min(jnp.where(logits == v1, lane, LANES), axis=-1, keepdims=True)
    rest = jnp.where(lane == i1, neg, logits)
    v2 = jnp.max(rest, axis=-1, keepdims=True)
    i2 = jnp.min(jnp.where(rest == v2, lane, LANES), axis=-1, keepdims=True)
    e2 = jnp.exp(v2 - v1)
    p1 = 1.0 / (1.0 + e2)
    p2 = e2 / (1.0 + e2)
    combine = jnp.where(lane == i1, p1, 0.0) + jnp.where(lane == i2, p2, 0.0)
    return u, combine


def _norm_mod_route(x, g, sc, sh, router, geo):
    tm = 256
    cond = _cond_of_block(geo, tm)
    n = x.shape[0]
    router_p = jnp.zeros((D_MODEL, LANES), F32).at[:, :N_EXPERTS].set(router)
    row = pl.BlockSpec((tm, D_MODEL), lambda i: (i, 0))
    cspec = pl.BlockSpec((None, 1, D_MODEL), lambda i: (cond(i), 0, 0))
    return pl.pallas_call(
        functools.partial(_rowwise_kernel, nin=5, fn=_norm_mod_route_fn),
        grid=(n // tm,),
        in_specs=[row, pl.BlockSpec((1, D_MODEL), lambda i: (0, 0)),
                  pl.BlockSpec((D_MODEL, LANES), lambda i: (0, 0)), cspec, cspec],
        out_specs=[row, pl.BlockSpec((tm, LANES), lambda i: (i, 0))],
        out_shape=[jax.ShapeDtypeStruct((n, D_MODEL), BF16), jax.ShapeDtypeStruct((n, LANES), F32)],
        compiler_params=pltpu.CompilerParams(dimension_semantics=("arbitrary",)),
    )(x, g, router_p, sc, sh)


def _dwconv_kernel(x_ref, w_ref, *rest, taps, geo, rb, post, has_bias):
    if has_bias:
        b_ref, o_ref, pad_ref = rest
    else:
        o_ref, pad_ref = rest
    half = (taps - 1) // 2
    tc = x_ref.shape[1]
    n_prompt_blocks = _tp(geo) // rb

    def run(nseq, ln):
        for s in range(nseq):
            base = s * ln
            zeros = jnp.zeros((CONV_HALO, tc), F32)
            pad_ref[0:CONV_HALO, :] = zeros
            pad_ref[CONV_HALO + ln:2 * CONV_HALO + ln, :] = zeros
            pad_ref[CONV_HALO:CONV_HALO + ln, :] = x_ref[base:base + ln, :]
            for r0 in range(0, ln, CONV_ROWS):
                acc = jnp.zeros((CONV_ROWS, tc), F32)
                for kk in range(taps):
                    st = CONV_HALO - half + kk + r0
                    acc = acc + w_ref[kk:kk + 1, :] * pad_ref[st:st + CONV_ROWS, :]
                if has_bias:
                    acc = acc + b_ref[...]
                o_ref[base + r0:base + r0 + CONV_ROWS, :] = post(acc)

    i = pl.program_id(0)
    if n_prompt_blocks > 0:
        @pl.when(i < n_prompt_blocks)
        def _():
            run(rb // geo.l_p, geo.l_p)

    @pl.when(i >= n_prompt_blocks)
    def _():
        run(1, geo.l_s)


def _dwconv(x, w, geo, *, n_ch, bias=None, post=None, tc=256):
    taps = w.shape[0]
    rb = geo.l_s
    t = x.shape[0]
    post = post or (lambda v: v)
    in_specs = [pl.BlockSpec((rb, tc), lambda i, j: (i, j)), pl.BlockSpec((taps, tc), lambda i, j: (0, j))]
    args = [x, w]
    if bias is not None:
        in_specs.append(pl.BlockSpec((1, tc), lambda i, j: (0, j)))
        args.append(bias)
    blk = _nbytes((rb, tc), F32)
    return pl.pallas_call(
        functools.partial(_dwconv_kernel, taps=taps, geo=geo, rb=rb, post=post, has_bias=bias is not None),
        grid=(t // rb, n_ch // tc), in_specs=in_specs,
        out_specs=pl.BlockSpec((rb, tc), lambda i, j: (i, j)),
        out_shape=jax.ShapeDtypeStruct((t, n_ch), F32),
        scratch_shapes=[pltpu.VMEM((rb + 2 * CONV_HALO, tc), F32)],
        compiler_params=pltpu.CompilerParams(
            dimension_semantics=("arbitrary", "arbitrary"), vmem_limit_bytes=_vmem_limit(blk, blk, blk)),
    )(*args)


def _dft_tables(ln):
    k = jnp.arange(ln, dtype=jnp.int32)[:, None]
    t = jnp.arange(ln, dtype=jnp.int32)[None, :]
    ang = ((k * t) % (2 * ln)).astype(F32) * (math.pi / ln)
    cos, sin = jnp.cos(ang), jnp.sin(ang)
    alt_t = jnp.where(t % 2 == 0, 1.0, -1.0).astype(F32)
    fwd_s = jnp.where(k == 0, alt_t, -sin)
    fwd = jnp.concatenate([cos, fwd_s], axis=0).astype(BF16)
    scale = 1.0 / (2 * ln)
    inv_c = jnp.where(k == 0, 1.0, 2.0 * cos).T * scale
    inv_s = jnp.where(k == 0, alt_t, -2.0 * sin).T * scale
    return fwd, inv_c.astype(BF16), inv_s.astype(BF16)


def _filter_kernel(w1_ref, b1_ref, w2_ref, b2_ref, w3_ref, hw_ref, ss_ref, *, ln):
    tn = hw_ref.shape[1]
    j = pl.program_id(0)
    pos = lax.broadcasted_iota(jnp.int32, (ln, LANES), 0).astype(F32) / ln
    lane = lax.broadcasted_iota(jnp.int32, (ln, LANES), 1)
    band = jnp.where(lane <= HY_BANDS, lane, lane - HY_BANDS).astype(F32)
    ang = (2.0 * math.pi) * pos * band
    feats = jnp.where(lane == 0, pos,
                      jnp.where(lane <= HY_BANDS, jnp.cos(ang),
                                jnp.where(lane <= 2 * HY_BANDS, jnp.sin(ang), 0.0)))
    h = jnp.sin(jnp.dot(feats, w1_ref[...], preferred_element_type=F32, precision=HIGHEST) + b1_ref[...])
    h = jnp.sin(jnp.dot(h, w2_ref[...], preferred_element_type=F32, precision=HIGHEST) + b2_ref[...])
    h = jnp.dot(h, w3_ref[...], preferred_element_type=F32, precision=HIGHEST)
    ch = (j * tn) % D_MODEL + lax.broadcasted_iota(jnp.int32, (1, tn), 1)
    step = (HY_FAST_DECAY - HY_SLOW_DECAY) / (D_MODEL - 1)
    deltas = HY_SLOW_DECAY + ch.astype(F32) * step
    tcol = lax.broadcasted_iota(jnp.int32, (ln, 1), 0).astype(F32) / ln
    h = h * (jnp.exp(-tcol * deltas) + HY_SHIFT)
    hw_ref[...] = h
    ss_ref[...] = jnp.sum(h * h, axis=0, keepdims=True)


def _hyena_filters(ln, w1, b1, w2, b2, w3, fwd):
    hid = w1.shape[1]
    ncol = w3.shape[1]
    tn = 512
    w1p = jnp.zeros((LANES, hid), F32).at[:w1.shape[0]].set(w1)
    full = lambda shape: pl.BlockSpec(shape, lambda j: (0, 0))
    hw, ss = pl.pallas_call(
        functools.partial(_filter_kernel, ln=ln),
        grid=(ncol // tn,),
        in_specs=[full((LANES, hid)), full((1, hid)), full((hid, hid)), full((1, hid)),
                  pl.BlockSpec((hid, tn), lambda j: (0, j))],
        out_specs=[pl.BlockSpec((ln, tn), lambda j: (0, j)), pl.BlockSpec((1, tn), lambda j: (0, j))],
        out_shape=[jax.ShapeDtypeStruct((ln, ncol), F32), jax.ShapeDtypeStruct((1, ncol), F32)],
        compiler_params=pltpu.CompilerParams(dimension_semantics=("arbitrary",),
                                             vmem_limit_bytes=_vmem_limit(_nbytes((ln, tn), F32) * 4)),
    )(w1p, b1[None, :], w2, b2[None, :], w3)
    spec = _linear(fwd, hw, out_dtype=F32, tm=_pick(2 * ln, (1024, 512)))
    half = ncol // 2
    tk = _pick(ln, (512, 256))

    def assemble(fc, fs, bc, bs, ssf, ssb):
        scale = lax.rsqrt(ssf + ssb + EPS)
        row0 = (pl.program_id(0) == 0) & (lax.broadcasted_iota(jnp.int32, fc.shape, 0) == 0)
        gc = (fc + bc) * scale
        gs = jnp.where(row0, fs + bs, fs - bs) * scale
        return gc, gs

    nk = ln // tk
    nj = half // tn
    return pl.pallas_call(
        functools.partial(_rowwise_kernel, nin=6, fn=assemble),
        grid=(nk, nj),
        in_specs=[pl.BlockSpec((tk, tn), lambda i, j: (i, j)),
                  pl.BlockSpec((tk, tn), lambda i, j: (nk + i, j)),
                  pl.BlockSpec((tk, tn), lambda i, j: (i, nj + j)),
                  pl.BlockSpec((tk, tn), lambda i, j: (nk + i, nj + j)),
                  pl.BlockSpec((1, tn), lambda i, j: (0, j)),
                  pl.BlockSpec((1, tn), lambda i, j: (0, nj + j))],
        out_specs=[pl.BlockSpec((tk, tn), lambda i, j: (i, j)), pl.BlockSpec((tk, tn), lambda i, j: (i, j))],
        out_shape=[jax.ShapeDtypeStruct((ln, half), F32)] * 2,
        compiler_params=pltpu.CompilerParams(dimension_semantics=("arbitrary", "arbitrary")),
    )(spec, spec, spec, spec, ss, ss)


def _long_conv_gated(z, z_col, pc, gate_col, gc, gs, order, bias, tables, *, ln, nseq, row0, out_dtype):
    fwd, inv_c, inv_s = tables
    d = D_MODEL
    tn = 512
    nj = d // tn
    tk = _pick(ln, (512, 256))
    nk = ln // tk
    z_row0 = row0 // ln if z.shape[0] != nseq * ln else 0

    def cmul(accs, ex):
        zc, zs = accs
        gcb, gsb = ex
        row_0 = (pl.program_id(2) == 0) & (lax.broadcasted_iota(jnp.int32, zc.shape, 0) == 0)
        yc = jnp.where(row_0, zc * gcb, zc * gcb - zs * gsb)
        ys = jnp.where(row_0, zs * gsb, zc * gsb + zs * gcb)
        return yc, ys

    yc, ys = _mm(
        [fwd, fwd], [z], [(0, 0), (1, 0)], grid=(nseq, nj, nk),
        a_specs=[pl.BlockSpec((tk, ln), lambda b, j, i: (i, 0)),
                 pl.BlockSpec((tk, ln), lambda b, j, i: (nk + i, 0))],
        b_specs=[pl.BlockSpec((ln, tn), lambda b, j, i: (z_row0 + b, z_col * nj + j))],
        out_shapes=[jax.ShapeDtypeStruct((nseq * ln, d), BF16)] * 2,
        out_specs=[pl.BlockSpec((tk, tn), lambda b, j, i: (b * nk + i, j))] * 2,
        epilogue=cmul, extras=[gc, gs],
        extra_specs=[pl.BlockSpec((tk, tn), lambda b, j, i: (i, order * nj + j))] * 2)

    tm = tk
    nm = ln // tm
    pc_rb = row0 // tm
    z_rb = z_row0 * nm

    def gate_ep(accs, ex):
        zin, gate, bb = ex
        return (gate * (accs[0] + accs[1] + zin * bb),)

    (out,) = _mm(
        [inv_c, inv_s], [yc, ys], [(0, 0), (1, 1)], grid=(nseq, nj, nm),
        a_specs=[pl.BlockSpec((tm, ln), lambda b, j, i: (i, 0))] * 2,
        b_specs=[pl.BlockSpec((ln, tn), lambda b, j, i: (b, j))] * 2,
        out_shapes=[jax.ShapeDtypeStruct((nseq * ln, d), out_dtype)],
        out_specs=[pl.BlockSpec((tm, tn), lambda b, j, i: (b * nm + i, j))],
        epilogue=gate_ep, extras=[z, pc, bias[:, None, :]],
        extra_specs=[pl.BlockSpec((tm, tn), lambda b, j, i: (z_rb + b * nm + i, z_col * nj + j)),
                     pl.BlockSpec((tm, tn), lambda b, j, i: (pc_rb + b * nm + i, gate_col * nj + j)),
                     pl.BlockSpec((None, 1, tn), lambda b, j, i: (order, 0, j))])
    return out


def _hyena_mixer(u, geo, w_in, short, w1, b1, w2, b2, w3, filt_bias, w_out, res_extras):
    d = D_MODEL
    p = _linear(u, w_in, out_dtype=F32)
    pc = _dwconv(p, short, geo, n_ch=3 * d)
    parts = []
    for ln, nseq, row0 in ((geo.l_p, geo.n_p, 0), (geo.l_s, geo.n_s, _tp(geo))):
        if nseq == 0:
            continue
        tables = _dft_tables(ln)
        gc, gs = _hyena_filters(ln, w1, b1, w2, b2, w3, tables[0])
        z1 = _long_conv_gated(pc, 0, pc, 1, gc, gs, 0, filt_bias, tables, ln=ln, nseq=nseq, row0=row0,
                              out_dtype=F32)
        z2 = _long_conv_gated(z1, 0, pc, 2, gc, gs, 1, filt_bias, tables, ln=ln, nseq=nseq, row0=row0,
                              out_dtype=BF16)
        parts.append(z2)
    z = jnp.concatenate(parts, axis=0) if len(parts) > 1 else parts[0]
    return _linear(z, w_out, out_dtype=F32, epilogue=_ep_residual, extras=res_extras,
                   extra_specs_fn=_residual_specs(geo), geo=geo)


def _head_linear(a, w, *, a_width_blocks=None, scale=None):
    t = a.shape[0]
    tm = _pick(t, (1024, 512, 256))
    w2 = w.reshape(ML_HEADS * ML_DH, ML_DH)
    ep = (lambda accs, ex: (accs[0] * scale,)) if scale is not None else (lambda accs, ex: (accs[0],))
    (out,) = _mm(
        [a], [w2], [(0, 0)], grid=(ML_HEADS, t // tm),
        a_specs=[pl.BlockSpec((tm, ML_DH), lambda h, i: (i, h))],
        b_specs=[pl.BlockSpec((ML_DH, ML_DH), lambda h, i: (h, 0))],
        out_shapes=[jax.ShapeDtypeStruct((t, ML_HEADS * ML_DH), BF16)],
        out_specs=[pl.BlockSpec((tm, ML_DH), lambda h, i: (i, h))],
        epilogue=ep)
    return out


def _gate_prep_kernel(pre_ref, o_ref):
    pre = pre_ref[...]
    lane = lax.broadcasted_iota(jnp.int32, pre.shape, 1)
    is_f = (lane % 16) >= ML_HEADS
    lf = jnp.minimum(pre, 0.0) - jnp.log(1.0 + jnp.exp(-jnp.abs(pre)))
    r = lax.broadcasted_iota(jnp.int32, (CHUNK, CHUNK), 0)
    c = lax.broadcasted_iota(jnp.int32, (CHUNK, CHUNK), 1)
    lower = (c <= r).astype(F32)
    upper = (c >= r).astype(F32)
    pre_sum = jnp.dot(lower, lf, preferred_element_type=F32, precision=HIGHEST)
    suf_sum = jnp.dot(upper, lf, preferred_element_type=F32, precision=HIGHEST)
    cum = jnp.where(lane < 16, pre_sum, suf_sum)
    o_ref[...] = jnp.where(is_f, cum, pre)


def _mlstm_scan_kernel(*refs, has_init, nc):
    if has_init:
        (q_ref, k_ref, v_ref, gc_ref, gr_ref, c0_ref, n0_ref, m0_ref,
         h_ref, co_ref, no_ref, mo_ref, c_s, n_s, m_s) = refs
    else:
        (q_ref, k_ref, v_ref, gc_ref, gr_ref, h_ref, co_ref, no_ref, mo_ref, c_s, n_s, m_s) = refs
    dr = pl.program_id(0)
    hh = pl.program_id(2)
    ci = pl.program_id(3)

    @pl.when(ci == 0)
    def _():
        if has_init:
            c_s[...] = c0_ref[...]
            n_s[...] = n0_ref[...]
            m_s[...] = m0_ref[...]
        else:
            c_s[...] = jnp.zeros_like(c_s)
            n_s[...] = jnp.zeros_like(n_s)
            m_s[...] = jnp.zeros_like(m_s)

    q = q_ref[...]
    k = k_ref[...]
    v = v_ref[...]
    li_idx = dr * 16 + hh
    b_idx = li_idx + ML_HEADS
    gcv = gc_ref[...]
    lane = lax.broadcasted_iota(jnp.int32, gcv.shape, 1)
    bcol = jnp.sum(jnp.where(lane == b_idx, gcv, 0.0), axis=1, keepdims=True)
    licol = jnp.sum(jnp.where(lane == li_idx, gcv, 0.0), axis=1, keepdims=True)
    brow = gr_ref[pl.ds(b_idx, 1), :]
    lirow = gr_ref[pl.ds(li_idx, 1), :]
    m_prev = m_s[...]
    c_prev = c_s[...]
    n_prev = n_s[...]

    rowi = lax.broadcasted_iota(jnp.int32, (CHUNK, CHUNK), 0)
    coli = lax.broadcasted_iota(jnp.int32, (CHUNK, CHUNK), 1)
    mask = (rowi - coli) * (1 - 2 * dr) >= 0
    dmat = jnp.where(mask, bcol - brow + lirow, -jnp.inf)
    a = bcol + m_prev
    mt = jnp.maximum(a, jnp.max(dmat, axis=1, keepdims=True))
    w_inter = jnp.exp(a - mt)
    qk = lax.dot_general(q, k, (((1,), (1,)), ((), ())), preferred_element_type=F32)
    s = qk * jnp.exp(dmat - mt)
    num = w_inter * jnp.dot(q, c_prev.astype(BF16), preferred_element_type=F32)
    num = num + jnp.dot(s.astype(BF16), v, preferred_element_type=F32)
    qn = jnp.sum(q.astype(F32) * n_prev, axis=1, keepdims=True)
    den = w_inter * qn + jnp.sum(s, axis=1, keepdims=True)
    h_ref[...] = num / jnp.maximum(jnp.abs(den), jnp.exp(-mt))

    b_last = jnp.min(bcol, axis=0, keepdims=True)
    dl = b_last - bcol + licol
    m_new = jnp.maximum(b_last + m_prev, jnp.max(dl, axis=0, keepdims=True))
    w_old = jnp.exp(b_last + m_prev - m_new)
    kw = k.astype(F32) * jnp.exp(dl - m_new)
    c_new = w_old * c_prev + lax.dot_general(kw.astype(BF16), v, (((0,), (0,)), ((), ())),
                                             preferred_element_type=F32)
    n_new = w_old * n_prev + jnp.sum(kw, axis=0, keepdims=True)
    c_s[...] = c_new
    n_s[...] = n_new
    m_s[...] = m_new

    @pl.when(ci == nc - 1)
    def _():
        co_ref[...] = c_new
        no_ref[...] = n_new
        mo_ref[...] = m_new


def _mlstm_scan(q, k, v, gcols, grows, *, ln, nseq, row0, init=None):
    nc = ln // CHUNK
    rb0 = row0 // CHUNK
    hd = ML_HEADS * ML_DH

    def blk(dr, b, h, c):
        return rb0 + b * nc + c + dr * (nc - 1 - 2 * c)

    row_spec = pl.BlockSpec((CHUNK, ML_DH), lambda dr, b, h, c: (blk(dr, b, h, c), h))
    in_specs = [row_spec, row_spec, row_spec,
                pl.BlockSpec((CHUNK, LANES), lambda dr, b, h, c: (blk(dr, b, h, c), 0)),
                pl.BlockSpec((LANES, CHUNK), lambda dr, b, h, c: (0, blk(dr, b, h, c)))]
    args = [q, k, v, gcols, grows]
    st_c = pl.BlockSpec((None, None, None, ML_DH, ML_DH), lambda dr, b, h, c: (b, dr, h, 0, 0))
    st_n = pl.BlockSpec((None, None, None, 1, ML_DH), lambda dr, b, h, c: (b, dr, h, 0, 0))
    st_m = pl.BlockSpec((None, None, None, 1, 1), lambda dr, b, h, c: (b, dr, h, 0, 0))
    if init is not None:
        in_specs += [st_c, st_n, st_m]
        args += list(init)
    return pl.pallas_call(
        functools.partial(_mlstm_scan_kernel, has_init=init is not None, nc=nc),
        grid=(2, nseq, ML_HEADS, nc), in_specs=in_specs,
        out_specs=[pl.BlockSpec((None, CHUNK, ML_DH),
                                lambda dr, b, h, c: (dr, b * nc + c + dr * (nc - 1 - 2 * c), h)),
                   st_c, st_n, st_m],
        out_shape=[jax.ShapeDtypeStruct((2, nseq * ln, hd), F32),
                   jax.ShapeDtypeStruct((nseq, 2, ML_HEADS, ML_DH, ML_DH), F32),
                   jax.ShapeDtypeStruct((nseq, 2, ML_HEADS, 1, ML_DH), F32),
                   jax.ShapeDtypeStruct((nseq, 2, ML_HEADS, 1, 1), F32)],
        scratch_shapes=[pltpu.VMEM((ML_DH, ML_DH), F32), pltpu.VMEM((1, ML_DH), F32), pltpu.VMEM((1, 1), F32)],
        compiler_params=pltpu.CompilerParams(dimension_semantics=("arbitrary",) * 4),
    )(*args)


def _mlstm_post_fn(hf, hb, xc, z, norm_g, skip):
    h = hf + hb
    outs = []
    for hd in range(ML_HEADS):
        sl = slice(hd * ML_DH, (hd + 1) * ML_DH)
        hh = h[:, sl]
        mu = jnp.mean(hh, axis=-1, keepdims=True)
        var = jnp.mean(jnp.square(hh - mu), axis=-1, keepdims=True)
        outs.append((hh - mu) * lax.rsqrt(var + EPS))
    hn = jnp.concatenate(outs, axis=-1)
    y = hn * norm_g + skip * xc
    return (y * _silu(z),)


def _mlstm_mixer(u, geo, state, w_up, conv_w, w_q, w_k, w_v, w_gate, b_gate, norm_g, skip, w_down,
                 res_extras):
    inner = ML_HEADS * ML_DH
    t = u.shape[0]
    up = _linear(u, w_up, out_dtype=F32)
    xc = _dwconv(up, conv_w, geo, n_ch=inner, post=_silu)
    q = _head_linear(xc, w_q, scale=ML_DH ** -0.5)
    k = _head_linear(xc, w_k)
    v = _head_linear(up, w_v)
    wg = jnp.transpose(w_gate, (1, 0, 2)).reshape(3 * inner, 4 * ML_HEADS)
    wg = jnp.zeros((3 * inner, LANES), F32).at[:, :4 * ML_HEADS].set(wg)
    bg = jnp.zeros((1, LANES), F32).at[0, :4 * ML_HEADS].set(b_gate.reshape(-1))
    tm = _pick(t, (1024, 512, 256))
    (pre,) = _mm(
        [q, k, v], [wg, wg, wg], [(0, 0), (1, 1), (2, 2)], grid=(1, t // tm),
        a_specs=[pl.BlockSpec((tm, inner), lambda j, i: (i, 0))] * 3,
        b_specs=[pl.BlockSpec((inner, LANES), lambda j, i, r=r: (r, 0)) for r in range(3)],
        out_shapes=[jax.ShapeDtypeStruct((t, LANES), F32)],
        out_specs=[pl.BlockSpec((tm, LANES), lambda j, i: (i, 0))],
        epilogue=lambda accs, ex: (accs[0] + accs[1] + accs[2] + ex[0],),
        extras=[bg], extra_specs=[pl.BlockSpec((1, LANES), lambda j, i: (0, 0))])
    gcols = pl.pallas_call(
        _gate_prep_kernel, grid=(t // CHUNK,),
        in_specs=[pl.BlockSpec((CHUNK, LANES), lambda i: (i, 0))],
        out_specs=pl.BlockSpec((CHUNK, LANES), lambda i: (i, 0)),
        out_shape=jax.ShapeDtypeStruct((t, LANES), F32),
        compiler_params=pltpu.CompilerParams(dimension_semantics=("arbitrary",)),
    )(pre)
    grows = gcols.T
    c0, n0, m0 = state
    hp, c_p, n_p, m_p = _mlstm_scan(q, k, v, gcols, grows, ln=geo.l_p, nseq=geo.n_p, row0=0)
    hs, _, _, _ = _mlstm_scan(q, k, v, gcols, grows, ln=geo.l_s, nseq=geo.n_s, row0=_tp(geo),
                              init=(c0, n0[:, :, :, None, :], m0[:, :, :, None, None]))
    ys = []
    for hdir, nrows, row0 in ((hp, _tp(geo), 0), (hs, t - _tp(geo), _tp(geo))):
        tmr = 256
        nb = nrows // tmr
        rb0 = row0 // tmr
        y = pl.pallas_call(
            functools.partial(_rowwise_kernel, nin=6, fn=_mlstm_post_fn),
            grid=(nb,),
            in_specs=[pl.BlockSpec((None, tmr, inner), lambda i: (0, i, 0)),
                      pl.BlockSpec((None, tmr, inner), lambda i: (1, i, 0)),
                      pl.BlockSpec((tmr, inner), lambda i: (rb0 + i, 0)),
                      pl.BlockSpec((tmr, inner), lambda i: (rb0 + i, 1)),
                      pl.BlockSpec((1, inner), lambda i: (0, 0)),
                      pl.BlockSpec((1, inner), lambda i: (0, 0))],
            out_specs=[pl.BlockSpec((tmr, inner), lambda i: (i, 0))],
            out_shape=[jax.ShapeDtypeStruct((nrows, inner), BF16)],
            compiler_params=pltpu.CompilerParams(
                dimension_semantics=("arbitrary",),
                vmem_limit_bytes=_vmem_limit(5 * _nbytes((tmr, inner), F32))),
        )(hdir, hdir, xc, up, norm_g[None, :], skip[None, :])[0]
        ys.append(y)
    y = jnp.concatenate(ys, axis=0)
    x_new = _linear(y, w_down, out_dtype=F32, epilogue=_ep_residual, extras=res_extras,
                    extra_specs_fn=_residual_specs(geo), geo=geo)
    return x_new, (c_p, n_p, m_p)


def _ln_silu_fn(y, g, b):
    mu = jnp.mean(y, axis=-1, keepdims=True)
    var = jnp.mean(jnp.square(y - mu), axis=-1, keepdims=True)
    return (_silu((y - mu) * lax.rsqrt(var + EPS) * g + b),)


def _conformer_mixer(u, geo, w1, b1, dw, b_dw, ln_g, ln_b, w2, b2, res_extras):
    d = D_MODEL
    b1r = b1[None, :]

    def glu(accs, ex):
        return ((accs[0] + ex[0]) * _sigmoid(accs[1] + ex[1]),)

    def especs(i_of, j_of, tm, tn):
        nj = d // tn
        return [pl.BlockSpec((1, tn), lambda *g: (0, j_of(*g))),
                pl.BlockSpec((1, tn), lambda *g: (0, nj + j_of(*g)))]

    y = _dual_linear(u, w1, w1, epilogue=glu, out_dtype=F32, n_out=d, w1_col=0, w2_col=1,
                     extras=[b1r, b1r], extra_specs_fn=especs)
    y = _dwconv(y, dw, geo, n_ch=d, bias=b_dw[None, :])
    (y,) = _rowwise(_ln_silu_fn, geo, rows=[(y, d, 0)], params=[(ln_g[None, :], d, 0), (ln_b[None, :], d, 0)],
                    outs=[(d, BF16)], tm=512)
    return _linear(y, w2, out_dtype=F32, epilogue=_ep_residual_bias, extras=list(res_extras) + [b2[None, :]],
                   extra_specs_fn=_residual_bias_specs(geo), geo=geo)


def _ep_swiglu(accs, ex):
    return (_silu(accs[0]) * accs[1],)


def _dense_ffn(u, geo, wg, wu, wd, res_extras):
    h = _dual_linear(u, wg, wu, epilogue=_ep_swiglu, out_dtype=BF16, n_out=wg.shape[1],
                     tn=_pick(wg.shape[1], (256, 128)))
    return _linear(h, wd, out_dtype=F32, epilogue=_ep_residual, extras=res_extras,
                   extra_specs_fn=_residual_specs(geo), geo=geo)


def _moe_down_kernel(h_ref, w_ref, cmb_ref, x_ref, g_ref, o_ref, acc_ref):
    e = pl.program_id(2)

    @pl.when(e == 0)
    def _():
        acc_ref[...] = jnp.zeros_like(acc_ref)

    cmb = cmb_ref[...]
    lane = lax.broadcasted_iota(jnp.int32, cmb.shape, 1)
    ce = jnp.sum(jnp.where(lane == e, cmb, 0.0), axis=1, keepdims=True)
    acc_ref[...] += ce * jnp.dot(h_ref[...], w_ref[...].astype(BF16), preferred_element_type=F32)

    @pl.when(e == N_EXPERTS - 1)
    def _():
        o_ref[...] = x_ref[...] + g_ref[...] * acc_ref[...]


def _moe_ffn(u, combine, geo, wg, wu, wd, res_extras):
    t = u.shape[0]
    ne, d, f = wg.shape
    tm = _row_tile(t, geo)
    tn = _pick(f, (512, 256, 128))
    nj = f // tn
    (h,) = _mm(
        [u], [wg, wu], [(0, 0), (0, 1)], grid=(ne, nj, t // tm),
        a_specs=[pl.BlockSpec((tm, d), lambda e, j, i: (i, 0))],
        b_specs=[pl.BlockSpec((None, d, tn), lambda e, j, i: (e, 0, j))] * 2,
        out_shapes=[jax.ShapeDtypeStruct((ne, t, f), BF16)],
        out_specs=[pl.BlockSpec((None, tm, tn), lambda e, j, i: (e, i, j))],
        epilogue=_ep_swiglu)
    x, gate = res_extras
    tno = 512
    cond = _cond_of_block(geo, tm)
    sizes = [_nbytes((tm, f), BF16), _nbytes((f, tno), F32), _nbytes((tm, tno), F32) * 4]
    return pl.pallas_call(
        _moe_down_kernel, grid=(d // tno, t // tm, ne),
        in_specs=[pl.BlockSpec((None, tm, f), lambda j, i, e: (e, i, 0)),
                  pl.BlockSpec((None, f, tno), lambda j, i, e: (e, 0, j)),
                  pl.BlockSpec((tm, LANES), lambda j, i, e: (i, 0)),
                  pl.BlockSpec((tm, tno), lambda j, i, e: (i, j)),
                  pl.BlockSpec((None, 1, tno), lambda j, i, e: (cond(i), 0, j))],
        out_specs=pl.BlockSpec((tm, tno), lambda j, i, e: (i, j)),
        out_shape=jax.ShapeDtypeStruct((t, d), F32),
        scratch_shapes=[pltpu.VMEM((tm, tno), F32)],
        compiler_params=pltpu.CompilerParams(
            dimension_semantics=("arbitrary",) * 3, vmem_limit_bytes=_vmem_limit(*sizes)),
    )(h, wd, combine, x, gate)


def _trunk(x, conds, geo, state, p):
    depth = p["w_ada"].shape[0]
    d = D_MODEL
    nco = conds.shape[0]
    cpad = jnp.zeros((8, d), F32).at[:nco].set(conds)
    w_ada = p["w_ada"]
    tn = 512
    nj = 6 * d // tn
    (mod,) = _mm(
        [cpad], [w_ada], [(0, 0)], grid=(depth, nj, 1),
        a_specs=[pl.BlockSpec((8, d), lambda l, j, i: (0, 0))],
        b_specs=[pl.BlockSpec((None, d, tn), lambda l, j, i: (l, 0, j))],
        out_shapes=[jax.ShapeDtypeStruct((depth, 8, 6 * d), F32)],
        out_specs=[pl.BlockSpec((None, 8, tn), lambda l, j, i: (l, 0, j))],
        epilogue=lambda accs, ex: (accs[0] + ex[0],), a_fn=_silu,
        extras=[p["b_ada"][:, None, :]],
        extra_specs=[pl.BlockSpec((None, 1, tn), lambda l, j, i: (l, 0, j))])
    mod = mod[:, :nco].reshape(depth, nco, 6, 1, d)

    new_states = []
    for i in range(depth):
        sh1, sc1, g1, sh2, sc2, g2 = (mod[i, :, r] for r in range(6))
        u = _norm_mod(x, p["norm_mix_g"][i][None, :], sc1, sh1, geo)
        kind, j = i % 3, i // 3
        res = [x, g1]
        if kind == 0:
            x = _hyena_mixer(u, geo, p["hy_w_in"][j], p["hy_short"][j], p["hy_filt_w1"][j], p["hy_filt_b1"][j],
                             p["hy_filt_w2"][j], p["hy_filt_b2"][j], p["hy_filt_w3"][j], p["hy_filt_bias"][j],
                             p["hy_w_out"][j], res)
        elif kind == 1:
            st = tuple(s[:, j] for s in state)
            x, new_st = _mlstm_mixer(u, geo, st, p["ml_w_up"][j], p["ml_conv"][j], p["ml_w_q"][j],
                                     p["ml_w_k"][j], p["ml_w_v"][j], p["ml_w_gate"][j], p["ml_b_gate"][j],
                                     p["ml_norm_g"][j], p["ml_skip"][j], p["ml_w_down"][j], res)
            new_states.append(new_st)
        else:
            x = _conformer_mixer(u, geo, p["cv_w1"][j], p["cv_b1"][j], p["cv_dw"][j], p["cv_b_dw"][j],
                                 p["cv_ln_g"][j], p["cv_ln_b"][j], p["cv_w2"][j], p["cv_b2"][j], res)
        jf = i // 2
        g_ffn = p["norm_ffn_g"][i][None, :]
        if i % 2 == 0:
            u = _norm_mod(x, g_ffn, sc2, sh2, geo)
            x = _dense_ffn(u, geo, p["ff_w_gate"][jf], p["ff_w_up"][jf], p["ff_w_down"][jf], [x, g2])
        else:
            u, combine = _norm_mod_route(x, g_ffn, sc2, sh2, p["moe_router"][jf], geo)
            x = _moe_ffn(u, combine, geo, p["moe_w_gate"][jf], p["moe_w_up"][jf], p["moe_w_down"][jf], [x, g2])
    fg = p["final_norm_g"][None, :]
    outs = []
    for nrows, row0 in ((_tp(geo), 0), (_tt(geo) - _tp(geo), _tp(geo))):
        tm = 512
        (y,) = _rowwise(lambda xb, g: (_rms(xb, g),), geo, rows=[(x, d, 0)], params=[(fg, d, 0)],
                        outs=[(d, F32)], tm=tm, n_rows=nrows, row_block0=row0 // tm)
        outs.append(y)
    new_state = tuple(jnp.stack(parts, axis=1) for parts in zip(*new_states))
    return outs[0], outs[1], new_state


def kernel(x_prompt, x_sample, state_mlstm_C, state_mlstm_n, state_mlstm_m, c, c_ctx, w_ada, b_ada, norm_mix_g, norm_ffn_g, hy_w_in, hy_short, hy_filt_w1, hy_filt_b1, hy_filt_w2, hy_filt_b2, hy_filt_w3, hy_filt_bias, hy_w_out, ml_w_up, ml_conv, ml_w_q, ml_w_k, ml_w_v, ml_w_gate, ml_b_gate, ml_norm_g, ml_skip, ml_w_down, cv_w1, cv_b1, cv_dw, cv_b_dw, cv_ln_g, cv_ln_b, cv_w2, cv_b2, ff_w_gate, ff_w_up, ff_w_down, moe_router, moe_w_gate, moe_w_up, moe_w_down, final_norm_g):
    p = dict(w_ada=w_ada, b_ada=b_ada, norm_mix_g=norm_mix_g, norm_ffn_g=norm_ffn_g, hy_w_in=hy_w_in,
             hy_short=hy_short, hy_filt_w1=hy_filt_w1, hy_filt_b1=hy_filt_b1, hy_filt_w2=hy_filt_w2,
             hy_filt_b2=hy_filt_b2, hy_filt_w3=hy_filt_w3, hy_filt_bias=hy_filt_bias, hy_w_out=hy_w_out,
             ml_w_up=ml_w_up, ml_conv=ml_conv, ml_w_q=ml_w_q, ml_w_k=ml_w_k, ml_w_v=ml_w_v,
             ml_w_gate=ml_w_gate, ml_b_gate=ml_b_gate, ml_norm_g=ml_norm_g, ml_skip=ml_skip,
             ml_w_down=ml_w_down, cv_w1=cv_w1, cv_b1=cv_b1, cv_dw=cv_dw, cv_b_dw=cv_b_dw, cv_ln_g=cv_ln_g,
             cv_ln_b=cv_ln_b, cv_w2=cv_w2, cv_b2=cv_b2, ff_w_gate=ff_w_gate, ff_w_up=ff_w_up,
             ff_w_down=ff_w_down, moe_router=moe_router, moe_w_gate=moe_w_gate, moe_w_up=moe_w_up,
             moe_w_down=moe_w_down, final_norm_g=final_norm_g)
    n_p, l_p, d = x_prompt.shape
    n_s, l_s, _ = x_sample.shape
    geo = Geo(n_p, l_p, n_s, l_s)
    x = jnp.concatenate([x_prompt.reshape(n_p * l_p, d), x_sample.reshape(n_s * l_s, d)], axis=0)
    conds = jnp.concatenate([c_ctx[None, :], c], axis=0)
    y_p, y_s, (c_new, n_new, m_new) = _trunk(
        x, conds, geo, (state_mlstm_C, state_mlstm_n, state_mlstm_m), p)
    n_ml = state_mlstm_C.shape[1]
    return (y_p.reshape(n_p, l_p, d), y_s.reshape(n_s, l_s, d),
            c_new.reshape(n_p, n_ml, 2, ML_HEADS, ML_DH, ML_DH),
            n_new.reshape(n_p, n_ml, 2, ML_HEADS, ML_DH),
            m_new.reshape(n_p, n_ml, 2, ML_HEADS))
```

```python
import collections
import functools
import math

import jax
import jax.numpy as jnp
from jax import lax
from jax.experimental import pallas as pl
from jax.experimental.pallas import tpu as pltpu

F32 = jnp.float32
BF16 = jnp.bfloat16
HIGHEST = lax.Precision.HIGHEST

D_MODEL = 1024
HY_BANDS = 16
HY_SLOW_DECAY = -math.log(1e-2) / 1.5
HY_FAST_DECAY = -math.log(1e-2) / 0.3
HY_SHIFT = 0.05
ML_HEADS = 8
ML_DH = 256
N_EXPERTS = 8
EPS = 1e-6

LANES = 128
CHUNK = 256
CONV_HALO = 16
CONV_ROWS = 64
VMEM_CAP = 56 * 1024 * 1024

Geo = collections.namedtuple("Geo", "n_p l_p n_s l_s")


def _tp(geo):
    return geo.n_p * geo.l_p


def _tt(geo):
    return geo.n_p * geo.l_p + geo.n_s * geo.l_s


def _cond_of_block(geo, rows):
    tp = _tp(geo)

    def f(i):
        r0 = i * rows
        return jnp.where(r0 < tp, 0, 1 + (r0 - tp) // geo.l_s)

    return f


def _vmem_limit(*nbytes):
    est = 2 * sum(nbytes) + (8 << 20)
    return int(min(max(est, 32 << 20), VMEM_CAP))


def _nbytes(shape, dtype):
    n = 1
    for s in shape:
        if s is not None:
            n *= s
    return n * jnp.dtype(dtype).itemsize


def _silu(x):
    return x * (1.0 / (1.0 + jnp.exp(-x)))


def _sigmoid(x):
    return 1.0 / (1.0 + jnp.exp(-x))


def _mm_kernel(*refs, na, nb, ne, no, pairs, cast_b, epilogue, a_fn, inner_axis):
    a_refs = refs[:na]
    b_refs = refs[na:na + nb]
    e_refs = refs[na + nb:na + nb + ne]
    o_refs = refs[na + nb + ne:na + nb + ne + no]
    scratch = refs[na + nb + ne + no:]
    first = pl.program_id(inner_axis) == 0
    b_src = []
    si = 0
    for j in range(nb):
        if cast_b[j]:
            s_ref = scratch[si]
            si += 1

            @pl.when(first)
            def _(s_ref=s_ref, b_ref=b_refs[j]):
                s_ref[...] = b_ref[...].astype(BF16)

            b_src.append(s_ref)
        else:
            b_src.append(b_refs[j])
    a_vals = {}
    accs = []
    for (i, j) in pairs:
        if i not in a_vals:
            a = a_refs[i][...]
            if a_fn is not None:
                a = a_fn(a)
            a_vals[i] = a.astype(BF16)
        accs.append(jnp.dot(a_vals[i], b_src[j][...], preferred_element_type=F32))
    outs = epilogue(accs, [e[...] for e in e_refs])
    for o_ref, val in zip(o_refs, outs):
        o_ref[...] = val.astype(o_ref.dtype)


def _mm(a_list, b_list, pairs, *, grid, a_specs, b_specs, out_shapes, out_specs,
        epilogue, extras=(), extra_specs=(), a_fn=None, name="mm"):
    cast_b = tuple(b.dtype != BF16 for b in b_list)
    scratch = [pltpu.VMEM(tuple(s for s in spec.block_shape if s is not None), BF16)
               for b, spec, c in zip(b_list, b_specs, cast_b) if c]
    kern = functools.partial(
        _mm_kernel, na=len(a_list), nb=len(b_list), ne=len(extras), no=len(out_shapes),
        pairs=tuple(pairs), cast_b=cast_b, epilogue=epilogue, a_fn=a_fn, inner_axis=len(grid) - 1)
    sizes = [_nbytes(s.block_shape, a.dtype) for a, s in zip(a_list, a_specs)]
    sizes += [_nbytes(s.block_shape, b.dtype) for b, s in zip(b_list, b_specs)]
    sizes += [_nbytes(s.block_shape, e.dtype) for e, s in zip(extras, extra_specs)]
    sizes += [_nbytes(s.block_shape, o.dtype) for o, s in zip(out_shapes, out_specs)]
    sizes += [_nbytes(s.block_shape, BF16) // 2 for s, c in zip(b_specs, cast_b) if c]
    res = pl.pallas_call(
        kern, grid=grid,
        in_specs=list(a_specs) + list(b_specs) + list(extra_specs),
        out_specs=list(out_specs), out_shape=list(out_shapes),
        scratch_shapes=scratch, name=name,
        compiler_params=pltpu.CompilerParams(
            dimension_semantics=("arbitrary",) * len(grid), vmem_limit_bytes=_vmem_limit(*sizes)),
    )(*a_list, *b_list, *extras)
    return res


def _pick(n, prefs):
    for p in prefs:
        if n % p == 0:
            return p
    return n


def _row_tile(m, geo):
    base = math.gcd(m, geo.l_s, _tp(geo) or m) if geo is not None else m
    return _pick(base, (1024, 512, 256, 128, 8))


def _linear(a, w, *, out_dtype, epilogue=None, extras=(), extra_specs_fn=None, tm=None, tn=None,
            a_col=0, w_col=0, w_row=0, n_out=None, a_fn=None, k=None, geo=None, name="linear"):
    m = a.shape[0]
    k = k or w.shape[0]
    n_out = n_out or w.shape[1]
    tm = tm or _row_tile(m, geo)
    tn = tn or _pick(n_out, (512, 256, 128))
    grid = (n_out // tn, m // tm)
    jo = w_col * (n_out // tn)
    especs = extra_specs_fn(lambda j, i: i, lambda j, i: j, tm, tn) if extras else ()
    ep = epilogue or (lambda accs, ex: (accs[0],))
    (out,) = _mm(
        [a], [w], [(0, 0)], grid=grid,
        a_specs=[pl.BlockSpec((tm, k), lambda j, i: (i, a_col))],
        b_specs=[pl.BlockSpec((k, tn), lambda j, i: (w_row, jo + j))],
        out_shapes=[jax.ShapeDtypeStruct((m, n_out), out_dtype)],
        out_specs=[pl.BlockSpec((tm, tn), lambda j, i: (i, j))],
        epilogue=ep, extras=extras, extra_specs=especs, a_fn=a_fn, name=name)
    return out


def _residual_specs(geo):
    def fn(i_of, j_of, tm, tn):
        cond = _cond_of_block(geo, tm)
        return [pl.BlockSpec((tm, tn), lambda *g: (i_of(*g), j_of(*g))),
                pl.BlockSpec((None, 1, tn), lambda *g: (cond(i_of(*g)), 0, j_of(*g)))]
    return fn


def _residual_bias_specs(geo):
    base = _residual_specs(geo)

    def fn(i_of, j_of, tm, tn):
        return base(i_of, j_of, tm, tn) + [pl.BlockSpec((1, tn), lambda *g: (0, j_of(*g)))]
    return fn


def _ep_residual(accs, ex):
    return (ex[0] + ex[1] * accs[0],)


def _ep_residual_bias(accs, ex):
    return (ex[0] + ex[1] * (accs[0] + ex[2]),)


def _dual_linear(a, w1, w2, *, epilogue, out_dtype, n_out, w1_col=0, w2_col=0, extras=(),
                 extra_specs_fn=None, tm=None, tn=None, geo=None, w_row=0, name="dual_linear"):
    m, k = a.shape
    tm = tm or _row_tile(m, geo)
    tn = tn or _pick(n_out, (512, 256, 128))
    nj = n_out // tn
    grid = (nj, m // tm)
    especs = extra_specs_fn(lambda j, i: i, lambda j, i: j, tm, tn) if extras else ()
    (out,) = _mm(
        [a], [w1, w2], [(0, 0), (0, 1)], grid=grid,
        a_specs=[pl.BlockSpec((tm, k), lambda j, i: (i, 0))],
        b_specs=[pl.BlockSpec((k, tn), lambda j, i: (w_row, w1_col * nj + j)),
                 pl.BlockSpec((k, tn), lambda j, i: (w_row, w2_col * nj + j))],
        out_shapes=[jax.ShapeDtypeStruct((m, n_out), out_dtype)],
        out_specs=[pl.BlockSpec((tm, tn), lambda j, i: (i, j))],
        epilogue=epilogue, extras=extras, extra_specs=especs, name=name)
    return out


def _rowwise_kernel(*refs, nin, fn):
    ins = [r[...] for r in refs[:nin]]
    outs = fn(*ins)
    for o_ref, val in zip(refs[nin:], outs):
        o_ref[...] = val.astype(o_ref.dtype)


def _rowwise(fn, geo, *, rows=(), params=(), conds=(), outs, tm, n_rows=None, row_block0=0, name="rowwise"):
    n_rows = n_rows or rows[0][0].shape[0]
    cond = _cond_of_block(geo, tm)
    in_specs, args, sizes = [], [], []
    for arr, w, cb in rows:
        in_specs.append(pl.BlockSpec((tm, w), lambda i, cb=cb: (i + row_block0, cb)))
        args.append(arr)
        sizes.append(_nbytes((tm, w), arr.dtype))
    for arr, w, cb in params:
        in_specs.append(pl.BlockSpec((1, w), lambda i, cb=cb: (0, cb)))
        args.append(arr)
    for arr in conds:
        w = arr.shape[-1]
        in_specs.append(pl.BlockSpec((None, 1, w), lambda i: (cond(i + row_block0), 0, 0)))
        args.append(arr)
    out_shapes = [jax.ShapeDtypeStruct((n_rows, w), dt) for w, dt in outs]
    out_specs = [pl.BlockSpec((tm, w), lambda i: (i, 0)) for w, dt in outs]
    sizes += [_nbytes((tm, w), dt) for w, dt in outs]
    return pl.pallas_call(
        functools.partial(_rowwise_kernel, nin=len(args), fn=fn),
        grid=(n_rows // tm,), in_specs=in_specs, out_specs=out_specs, out_shape=out_shapes, name=name,
        compiler_params=pltpu.CompilerParams(
            dimension_semantics=("arbitrary",), vmem_limit_bytes=_vmem_limit(*sizes, *sizes)),
    )(*args)


def _rms(x, g):
    return x * lax.rsqrt(jnp.mean(x * x, axis=-1, keepdims=True) + EPS) * g


def _norm_mod_fn(x, g, sc, sh):
    return (_rms(x, g) * (1.0 + sc) + sh,)


def _norm_mod(x, g, sc, sh, geo):
    (u,) = _rowwise(_norm_mod_fn, geo, rows=[(x, D_MODEL, 0)], params=[(g, D_MODEL, 0)],
                    conds=[sc, sh], outs=[(D_MODEL, BF16)], tm=512, name="norm_mod")
    return u


def _norm_mod_route_fn(x, g, router, sc, sh):
    u = _rms(x, g) * (1.0 + sc) + sh
    logits = jnp.dot(u, router, preferred_element_type=F32, precision=HIGHEST)
    lane = lax.broadcasted_iota(jnp.int32, logits.shape, 1)
    neg = jnp.float32(-jnp.inf)
    logits = jnp.where(lane < N_EXPERTS, logits, neg)
    v1 = jnp.max(logits, axis=-1, keepdims=True)
    i1 = jnp.min(jnp.where(logits == v1, lane, LANES), axis=-1, keepdims=True)
    rest = jnp.where(lane == i1, neg, logits)
    v2 = jnp.max(rest, axis=-1, keepdims=True)
    i2 = jnp.min(jnp.where(rest == v2, lane, LANES), axis=-1, keepdims=True)
    e2 = jnp.exp(v2 - v1)
    p1 = 1.0 / (1.0 + e2)
    p2 = e2 / (1.0 + e2)
    combine = jnp.where(lane == i1, p1, 0.0) + jnp.where(lane == i2, p2, 0.0)
    return u, combine


def _norm_mod_route(x, g, sc, sh, router, geo):
    tm = 256
    cond = _cond_of_block(geo, tm)
    n = x.shape[0]
    router_p = jnp.zeros((D_MODEL, LANES), F32).at[:, :N_EXPERTS].set(router)
    row = pl.BlockSpec((tm, D_MODEL), lambda i: (i, 0))
    cspec = pl.BlockSpec((None, 1, D_MODEL), lambda i: (cond(i), 0, 0))
    return pl.pallas_call(
        functools.partial(_rowwise_kernel, nin=5, fn=_norm_mod_route_fn),
        grid=(n // tm,),
        in_specs=[row, pl.BlockSpec((1, D_MODEL), lambda i: (0, 0)),
                  pl.BlockSpec((D_MODEL, LANES), lambda i: (0, 0)), cspec, cspec],
        out_specs=[row, pl.BlockSpec((tm, LANES), lambda i: (i, 0))],
        out_shape=[jax.ShapeDtypeStruct((n, D_MODEL), BF16), jax.ShapeDtypeStruct((n, LANES), F32)],
        name="norm_mod_route", compiler_params=pltpu.CompilerParams(dimension_semantics=("arbitrary",)),
    )(x, g, router_p, sc, sh)


def _dwconv_kernel(x_ref, w_ref, *rest, taps, geo, rb, post, has_bias):
    if has_bias:
        b_ref, o_ref, pad_ref = rest
    else:
        o_ref, pad_ref = rest
    half = (taps - 1) // 2
    tc = x_ref.shape[1]
    n_prompt_blocks = _tp(geo) // rb

    def run(nseq, ln):
        for s in range(nseq):
            base = s * ln
            zeros = jnp.zeros((CONV_HALO, tc), F32)
            pad_ref[0:CONV_HALO, :] = zeros
            pad_ref[CONV_HALO + ln:2 * CONV_HALO + ln, :] = zeros
            pad_ref[CONV_HALO:CONV_HALO + ln, :] = x_ref[base:base + ln, :]
            for r0 in range(0, ln, CONV_ROWS):
                acc = jnp.zeros((CONV_ROWS, tc), F32)
                for kk in range(taps):
                    st = CONV_HALO - half + kk + r0
                    acc = acc + w_ref[kk:kk + 1, :] * pad_ref[st:st + CONV_ROWS, :]
                if has_bias:
                    acc = acc + b_ref[...]
                o_ref[base + r0:base + r0 + CONV_ROWS, :] = post(acc)

    i = pl.program_id(0)
    if n_prompt_blocks > 0:
        @pl.when(i < n_prompt_blocks)
        def _():
            run(rb // geo.l_p, geo.l_p)

    @pl.when(i >= n_prompt_blocks)
    def _():
        run(1, geo.l_s)


def _dwconv(x, w, geo, *, n_ch, bias=None, post=None, tc=256):
    taps = w.shape[0]
    rb = geo.l_s
    t = x.shape[0]
    post = post or (lambda v: v)
    in_specs = [pl.BlockSpec((rb, tc), lambda i, j: (i, j)), pl.BlockSpec((taps, tc), lambda i, j: (0, j))]
    args = [x, w]
    if bias is not None:
        in_specs.append(pl.BlockSpec((1, tc), lambda i, j: (0, j)))
        args.append(bias)
    blk = _nbytes((rb, tc), F32)
    return pl.pallas_call(
        functools.partial(_dwconv_kernel, taps=taps, geo=geo, rb=rb, post=post, has_bias=bias is not None),
        grid=(t // rb, n_ch // tc), in_specs=in_specs,
        out_specs=pl.BlockSpec((rb, tc), lambda i, j: (i, j)),
        out_shape=jax.ShapeDtypeStruct((t, n_ch), F32),
        scratch_shapes=[pltpu.VMEM((rb + 2 * CONV_HALO, tc), F32)], name="dwconv%d" % taps,
        compiler_params=pltpu.CompilerParams(
            dimension_semantics=("arbitrary", "arbitrary"), vmem_limit_bytes=_vmem_limit(blk, blk, blk)),
    )(*args)


def _dft_tables(ln):
    kb = 64
    t3 = jnp.arange(ln, dtype=jnp.int32)[None, None, :]
    a3 = jnp.arange(ln // kb, dtype=jnp.int32)[:, None, None]
    b3 = jnp.arange(kb, dtype=jnp.int32)[None, :, None]
    ang_a = ((kb * a3 * t3) % (2 * ln)).astype(F32) * (math.pi / ln)
    ang_b = ((b3 * t3) % (2 * ln)).astype(F32) * (math.pi / ln)
    ca, sa, cb, sb = jnp.cos(ang_a), jnp.sin(ang_a), jnp.cos(ang_b), jnp.sin(ang_b)
    cos = (ca * cb - sa * sb).reshape(ln, ln)
    sin = (sa * cb + ca * sb).reshape(ln, ln)
    k = jnp.arange(ln, dtype=jnp.int32)[:, None]
    t = jnp.arange(ln, dtype=jnp.int32)[None, :]
    alt_t = jnp.where(t % 2 == 0, 1.0, -1.0).astype(F32)
    fwd_s = jnp.where(k == 0, alt_t, -sin)
    fwd = jnp.concatenate([cos, fwd_s], axis=0).astype(BF16)
    scale = 1.0 / (2 * ln)
    inv_c = jnp.where(k == 0, 1.0, 2.0 * cos).T * scale
    inv_s = jnp.where(k == 0, alt_t, -2.0 * sin).T * scale
    return fwd, inv_c.astype(BF16), inv_s.astype(BF16)


def _filter_hidden_kernel(w1_ref, b1_ref, w2_ref, b2_ref, o_ref, *, ln):
    pos = lax.broadcasted_iota(jnp.int32, (ln, LANES), 0).astype(F32) / ln
    lane = lax.broadcasted_iota(jnp.int32, (ln, LANES), 1)
    band = jnp.where(lane <= HY_BANDS, lane, lane - HY_BANDS).astype(F32)
    ang = (2.0 * math.pi) * pos * band
    feats = jnp.where(lane == 0, pos,
                      jnp.where(lane <= HY_BANDS, jnp.cos(ang),
                                jnp.where(lane <= 2 * HY_BANDS, jnp.sin(ang), 0.0)))
    h = jnp.sin(jnp.dot(feats, w1_ref[...], preferred_element_type=F32, precision=HIGHEST) + b1_ref[...])
    o_ref[...] = jnp.sin(jnp.dot(h, w2_ref[...], preferred_element_type=F32, precision=HIGHEST) + b2_ref[...])


def _filter_kernel(h_ref, w3_ref, hw_ref, ss_ref, *, ln):
    tn = hw_ref.shape[1]
    j = pl.program_id(0)
    h = jnp.dot(h_ref[...], w3_ref[...], preferred_element_type=F32, precision=HIGHEST)
    ch = (j * tn) % D_MODEL + lax.broadcasted_iota(jnp.int32, (1, tn), 1)
    step = (HY_FAST_DECAY - HY_SLOW_DECAY) / (D_MODEL - 1)
    deltas = HY_SLOW_DECAY + ch.astype(F32) * step
    tcol = lax.broadcasted_iota(jnp.int32, (ln, 1), 0).astype(F32) / ln
    h = h * (jnp.exp(-tcol * deltas) + HY_SHIFT)
    hw_ref[...] = h
    ss_ref[...] = jnp.sum(h * h, axis=0, keepdims=True)


def _hyena_filters(ln, w1, b1, w2, b2, w3, fwd):
    hid = w1.shape[1]
    ncol = w3.shape[1]
    tn = 512
    w1p = jnp.zeros((LANES, hid), F32).at[:w1.shape[0]].set(w1)
    full = lambda shape: pl.BlockSpec(shape, lambda j: (0, 0))
    hidden = pl.pallas_call(
        functools.partial(_filter_hidden_kernel, ln=ln),
        grid=(1,),
        in_specs=[full((LANES, hid)), full((1, hid)), full((hid, hid)), full((1, hid))],
        out_specs=full((ln, hid)),
        out_shape=jax.ShapeDtypeStruct((ln, hid), F32),
        name="hy_filter_hidden",
        compiler_params=pltpu.CompilerParams(dimension_semantics=("arbitrary",)),
    )(w1p, b1[None, :], w2, b2[None, :])
    hw, ss = pl.pallas_call(
        functools.partial(_filter_kernel, ln=ln),
        grid=(ncol // tn,),
        in_specs=[full((ln, hid)), pl.BlockSpec((hid, tn), lambda j: (0, j))],
        out_specs=[pl.BlockSpec((ln, tn), lambda j: (0, j)), pl.BlockSpec((1, tn), lambda j: (0, j))],
        out_shape=[jax.ShapeDtypeStruct((ln, ncol), F32), jax.ShapeDtypeStruct((1, ncol), F32)],
        name="hy_filter_cols",
        compiler_params=pltpu.CompilerParams(dimension_semantics=("arbitrary",),
                                             vmem_limit_bytes=_vmem_limit(_nbytes((ln, tn), F32) * 4)),
    )(hidden, w3)
    spec = _linear(fwd, hw, out_dtype=F32, tm=_pick(2 * ln, (1024, 512)), name="hy_filter_dft")
    half = ncol // 2
    tk = _pick(ln, (512, 256))

    def assemble(fc, fs, bc, bs, ssf, ssb):
        scale = lax.rsqrt(ssf + ssb + EPS)
        row0 = (pl.program_id(0) == 0) & (lax.broadcasted_iota(jnp.int32, fc.shape, 0) == 0)
        gc = (fc + bc) * scale
        gs = jnp.where(row0, fs + bs, fs - bs) * scale
        return gc, gs

    nk = ln // tk
    nj = half // tn
    return pl.pallas_call(
        functools.partial(_rowwise_kernel, nin=6, fn=assemble),
        grid=(nk, nj),
        in_specs=[pl.BlockSpec((tk, tn), lambda i, j: (i, j)),
                  pl.BlockSpec((tk, tn), lambda i, j: (nk + i, j)),
                  pl.BlockSpec((tk, tn), lambda i, j: (i, nj + j)),
                  pl.BlockSpec((tk, tn), lambda i, j: (nk + i, nj + j)),
                  pl.BlockSpec((1, tn), lambda i, j: (0, j)),
                  pl.BlockSpec((1, tn), lambda i, j: (0, nj + j))],
        out_specs=[pl.BlockSpec((tk, tn), lambda i, j: (i, j)), pl.BlockSpec((tk, tn), lambda i, j: (i, j))],
        out_shape=[jax.ShapeDtypeStruct((ln, half), F32)] * 2, name="hy_filter_assemble",
        compiler_params=pltpu.CompilerParams(dimension_semantics=("arbitrary", "arbitrary")),
    )(spec, spec, spec, spec, ss, ss)


def _dft_fwd_kernel(fc_ref, fs_ref, z_ref, gc_ref, gs_ref, yc_ref, ys_ref, zb_ref, *, nb):
    @pl.when(pl.program_id(2) == 0)
    def _():
        zb_ref[...] = z_ref[...].astype(BF16)

    fc = fc_ref[...]
    fs = fs_ref[...]
    gcb = gc_ref[...]
    gsb = gs_ref[...]
    row_0 = (pl.program_id(2) == 0) & (lax.broadcasted_iota(jnp.int32, gcb.shape, 0) == 0)
    for s in range(nb):
        zc = jnp.dot(fc, zb_ref[s], preferred_element_type=F32)
        zs = jnp.dot(fs, zb_ref[s], preferred_element_type=F32)
        yc_ref[s] = jnp.where(row_0, zc * gcb, zc * gcb - zs * gsb).astype(yc_ref.dtype)
        ys_ref[s] = jnp.where(row_0, zs * gsb, zc * gsb + zs * gcb).astype(ys_ref.dtype)


def _dft_inv_kernel(ic_ref, is_ref, yc_ref, ys_ref, zin_ref, gate_ref, bias_ref, o_ref, *, nb):
    ic = ic_ref[...]
    isn = is_ref[...]
    bias = bias_ref[...]
    for s in range(nb):
        acc = jnp.dot(ic, yc_ref[s], preferred_element_type=F32)
        acc = acc + jnp.dot(isn, ys_ref[s], preferred_element_type=F32)
        o_ref[s] = (gate_ref[s] * (acc + zin_ref[s] * bias)).astype(o_ref.dtype)


def _long_conv_gated(z, z_col, pc, gate_col, gc, gs, order, bias, tables, *, ln, nseq, row0, out_dtype):
    fwd, inv_c, inv_s = tables
    d = D_MODEL
    tn = 512
    nj = d // tn
    tk = _pick(ln, (512, 256))
    nk = ln // tk
    nb = max(n for n in (8, 4, 2, 1) if nseq % n == 0 and n * ln <= 2048)
    z3 = z.reshape(z.shape[0] // ln, ln, z.shape[1])
    pc3 = pc.reshape(pc.shape[0] // ln, ln, pc.shape[1])
    z_sb0 = (row0 // ln if z.shape[0] != nseq * ln else 0) // nb
    pc_sb0 = (row0 // ln) // nb
    seq_blk = (nb, ln, tn)
    out_blk = (nb, tk, tn)
    cparams = pltpu.CompilerParams(
        dimension_semantics=("arbitrary",) * 3,
        vmem_limit_bytes=_vmem_limit(_nbytes(seq_blk, F32) * 2, _nbytes(out_blk, F32) * 3, _nbytes((tk, ln), BF16) * 2))
    yc, ys = pl.pallas_call(
        functools.partial(_dft_fwd_kernel, nb=nb), grid=(nseq // nb, nj, nk),
        in_specs=[pl.BlockSpec((tk, ln), lambda b, j, i: (i, 0)),
                  pl.BlockSpec((tk, ln), lambda b, j, i: (nk + i, 0)),
                  pl.BlockSpec(seq_blk, lambda b, j, i: (z_sb0 + b, 0, z_col * nj + j)),
                  pl.BlockSpec((tk, tn), lambda b, j, i: (i, order * nj + j)),
                  pl.BlockSpec((tk, tn), lambda b, j, i: (i, order * nj + j))],
        out_specs=[pl.BlockSpec(out_blk, lambda b, j, i: (b, i, j))] * 2,
        out_shape=[jax.ShapeDtypeStruct((nseq, ln, d), BF16)] * 2,
        scratch_shapes=[pltpu.VMEM(seq_blk, BF16)], name="hy_dft_fwd", compiler_params=cparams,
    )(fwd, fwd, z3, gc, gs)
    out = pl.pallas_call(
        functools.partial(_dft_inv_kernel, nb=nb), grid=(nseq // nb, nj, nk),
        in_specs=[pl.BlockSpec((tk, ln), lambda b, j, i: (i, 0)),
                  pl.BlockSpec((tk, ln), lambda b, j, i: (i, 0)),
                  pl.BlockSpec(seq_blk, lambda b, j, i: (b, 0, j)),
                  pl.BlockSpec(seq_blk, lambda b, j, i: (b, 0, j)),
                  pl.BlockSpec(out_blk, lambda b, j, i: (z_sb0 + b, i, z_col * nj + j)),
                  pl.BlockSpec(out_blk, lambda b, j, i: (pc_sb0 + b, i, gate_col * nj + j)),
                  pl.BlockSpec((None, 1, tn), lambda b, j, i: (order, 0, j))],
        out_specs=pl.BlockSpec(out_blk, lambda b, j, i: (b, i, j)),
        out_shape=jax.ShapeDtypeStruct((nseq, ln, d), out_dtype),
        name="hy_dft_inv", compiler_params=cparams,
    )(inv_c, inv_s, yc, ys, z3, pc3, bias[:, None, :])
    return out.reshape(nseq * ln, d)


def _hyena_mixer(u, geo, w_in, short, w1, b1, w2, b2, w3, filt_bias, w_out, res_extras):
    d = D_MODEL
    p = _linear(u, w_in, out_dtype=F32, name="hy_in")
    pc = _dwconv(p, short, geo, n_ch=3 * d)
    parts = []
    for ln, nseq, row0 in ((geo.l_p, geo.n_p, 0), (geo.l_s, geo.n_s, _tp(geo))):
        if nseq == 0:
            continue
        tables = _dft_tables(ln)
        gc, gs = _hyena_filters(ln, w1, b1, w2, b2, w3, tables[0])
        z1 = _long_conv_gated(pc, 0, pc, 1, gc, gs, 0, filt_bias, tables, ln=ln, nseq=nseq, row0=row0,
                              out_dtype=F32)
        z2 = _long_conv_gated(z1, 0, pc, 2, gc, gs, 1, filt_bias, tables, ln=ln, nseq=nseq, row0=row0,
                              out_dtype=BF16)
        parts.append(z2)
    z = jnp.concatenate(parts, axis=0) if len(parts) > 1 else parts[0]
    return _linear(z, w_out, out_dtype=F32, epilogue=_ep_residual, extras=res_extras,
                   extra_specs_fn=_residual_specs(geo), geo=geo, name="hy_out")


def _head_linear(a, w, *, a_width_blocks=None, scale=None):
    t = a.shape[0]
    tm = _pick(t, (1024, 512, 256))
    w2 = w.reshape(ML_HEADS * ML_DH, ML_DH)
    ep = (lambda accs, ex: (accs[0] * scale,)) if scale is not None else (lambda accs, ex: (accs[0],))
    (out,) = _mm(
        [a], [w2], [(0, 0)], grid=(ML_HEADS, t // tm),
        a_specs=[pl.BlockSpec((tm, ML_DH), lambda h, i: (i, h))],
        b_specs=[pl.BlockSpec((ML_DH, ML_DH), lambda h, i: (h, 0))],
        out_shapes=[jax.ShapeDtypeStruct((t, ML_HEADS * ML_DH), BF16)],
        out_specs=[pl.BlockSpec((tm, ML_DH), lambda h, i: (i, h))],
        epilogue=ep, name="ml_head_linear")
    return out


def _gate_prep_kernel(pre_ref, o_ref):
    pre = pre_ref[...]
    lane = lax.broadcasted_iota(jnp.int32, pre.shape, 1)
    is_f = (lane % 16) >= ML_HEADS
    lf = jnp.minimum(pre, 0.0) - jnp.log(1.0 + jnp.exp(-jnp.abs(pre)))
    r = lax.broadcasted_iota(jnp.int32, (CHUNK, CHUNK), 0)
    c = lax.broadcasted_iota(jnp.int32, (CHUNK, CHUNK), 1)
    lower = (c <= r).astype(F32)
    upper = (c >= r).astype(F32)
    pre_sum = jnp.dot(lower, lf, preferred_element_type=F32, precision=HIGHEST)
    suf_sum = jnp.dot(upper, lf, preferred_element_type=F32, precision=HIGHEST)
    cum = jnp.where(lane < 16, pre_sum, suf_sum)
    o_ref[...] = jnp.where(is_f, cum, pre)


def _mlstm_scan_kernel(*refs, has_init, nc):
    if has_init:
        (q_ref, k_ref, v_ref, gc_ref, gr_ref, c0_ref, n0_ref, m0_ref,
         h_ref, co_ref, no_ref, mo_ref, c_s, n_s, m_s) = refs
    else:
        (q_ref, k_ref, v_ref, gc_ref, gr_ref, h_ref, co_ref, no_ref, mo_ref, c_s, n_s, m_s) = refs
    dr = pl.program_id(0)
    hh = pl.program_id(2)
    ci = pl.program_id(3)

    carry = has_init or nc > 1
    if carry:
        @pl.when(ci == 0)
        def _():
            if has_init:
                c_s[...] = c0_ref[...]
                n_s[...] = n0_ref[...]
                m_s[...] = m0_ref[...]
            else:
                c_s[...] = jnp.zeros_like(c_s)
                n_s[...] = jnp.zeros_like(n_s)
                m_s[...] = jnp.zeros_like(m_s)

    q = q_ref[...]
    k = k_ref[...]
    v = v_ref[...]
    li_idx = dr * 16 + hh
    b_idx = li_idx + ML_HEADS
    gcv = gc_ref[...]
    lane = lax.broadcasted_iota(jnp.int32, gcv.shape, 1)
    bcol = jnp.sum(jnp.where(lane == b_idx, gcv, 0.0), axis=1, keepdims=True)
    licol = jnp.sum(jnp.where(lane == li_idx, gcv, 0.0), axis=1, keepdims=True)
    brow = gr_ref[pl.ds(b_idx, 1), :]
    lirow = gr_ref[pl.ds(li_idx, 1), :]
    m_prev = m_s[...] if carry else jnp.zeros((1, 1), F32)

    rowi = lax.broadcasted_iota(jnp.int32, (CHUNK, CHUNK), 0)
    coli = lax.broadcasted_iota(jnp.int32, (CHUNK, CHUNK), 1)
    mask = (rowi - coli) * (1 - 2 * dr) >= 0
    dmat = jnp.where(mask, bcol - brow + lirow, -jnp.inf)
    a = bcol + m_prev
    mt = jnp.maximum(a, jnp.max(dmat, axis=1, keepdims=True))
    w_inter = jnp.exp(a - mt)
    qk = lax.dot_general(q, k, (((1,), (1,)), ((), ())), preferred_element_type=F32)
    s = qk * jnp.exp(dmat - mt)
    num = jnp.dot(s.astype(BF16), v, preferred_element_type=F32)
    den = jnp.sum(s, axis=1, keepdims=True)
    if carry:
        c_prev = c_s[...]
        n_prev = n_s[...]
        num = num + w_inter * jnp.dot(q, c_prev.astype(BF16), preferred_element_type=F32)
        den = den + w_inter * jnp.sum(q.astype(F32) * n_prev, axis=1, keepdims=True)
    h_ref[...] = num / jnp.maximum(jnp.abs(den), jnp.exp(-mt))

    b_last = jnp.min(bcol, axis=0, keepdims=True)
    dl = b_last - bcol + licol
    m_new = jnp.maximum(b_last + m_prev, jnp.max(dl, axis=0, keepdims=True))
    kw = k.astype(F32) * jnp.exp(dl - m_new)
    c_new = lax.dot_general(kw.astype(BF16), v, (((0,), (0,)), ((), ())), preferred_element_type=F32)
    n_new = jnp.sum(kw, axis=0, keepdims=True)
    if carry:
        w_old = jnp.exp(b_last + m_prev - m_new)
        c_new = w_old * c_prev + c_new
        n_new = w_old * n_prev + n_new
        c_s[...] = c_new
        n_s[...] = n_new
        m_s[...] = m_new

    @pl.when(ci == nc - 1)
    def _():
        co_ref[...] = c_new
        no_ref[...] = n_new
        mo_ref[...] = m_new


def _mlstm_scan(q, k, v, gcols, grows, *, ln, nseq, row0, init=None):
    nc = ln // CHUNK
    rb0 = row0 // CHUNK
    hd = ML_HEADS * ML_DH

    def blk(dr, b, h, c):
        return rb0 + b * nc + c + dr * (nc - 1 - 2 * c)

    row_spec = pl.BlockSpec((CHUNK, ML_DH), lambda dr, b, h, c: (blk(dr, b, h, c), h))
    in_specs = [row_spec, row_spec, row_spec,
                pl.BlockSpec((CHUNK, LANES), lambda dr, b, h, c: (blk(dr, b, h, c), 0)),
                pl.BlockSpec((LANES, CHUNK), lambda dr, b, h, c: (0, blk(dr, b, h, c)))]
    args = [q, k, v, gcols, grows]
    st_c = pl.BlockSpec((None, None, None, ML_DH, ML_DH), lambda dr, b, h, c: (b, dr, h, 0, 0))
    st_n = pl.BlockSpec((None, None, None, 1, ML_DH), lambda dr, b, h, c: (b, dr, h, 0, 0))
    st_m = pl.BlockSpec((None, None, None, 1, 1), lambda dr, b, h, c: (b, dr, h, 0, 0))
    if init is not None:
        in_specs += [st_c, st_n, st_m]
        args += list(init)
    return pl.pallas_call(
        functools.partial(_mlstm_scan_kernel, has_init=init is not None, nc=nc),
        grid=(2, nseq, ML_HEADS, nc), in_specs=in_specs,
        out_specs=[pl.BlockSpec((None, CHUNK, ML_DH),
                                lambda dr, b, h, c: (dr, b * nc + c + dr * (nc - 1 - 2 * c), h)),
                   st_c, st_n, st_m],
        out_shape=[jax.ShapeDtypeStruct((2, nseq * ln, hd), F32),
                   jax.ShapeDtypeStruct((nseq, 2, ML_HEADS, ML_DH, ML_DH), F32),
                   jax.ShapeDtypeStruct((nseq, 2, ML_HEADS, 1, ML_DH), F32),
                   jax.ShapeDtypeStruct((nseq, 2, ML_HEADS, 1, 1), F32)],
        scratch_shapes=[pltpu.VMEM((ML_DH, ML_DH), F32), pltpu.VMEM((1, ML_DH), F32), pltpu.VMEM((1, 1), F32)],
        name="mlstm_scan_l%d" % ln, compiler_params=pltpu.CompilerParams(dimension_semantics=("arbitrary",) * 4),
    )(*args)


def _mlstm_post_fn(hf, hb, xc, z, norm_g, skip):
    h = hf + hb
    outs = []
    for hd in range(ML_HEADS):
        sl = slice(hd * ML_DH, (hd + 1) * ML_DH)
        hh = h[:, sl]
        mu = jnp.mean(hh, axis=-1, keepdims=True)
        var = jnp.mean(jnp.square(hh - mu), axis=-1, keepdims=True)
        outs.append((hh - mu) * lax.rsqrt(var + EPS))
    hn = jnp.concatenate(outs, axis=-1)
    y = hn * norm_g + skip * xc
    return (y * _silu(z),)


def _mlstm_mixer(u, geo, state, w_up, conv_w, w_q, w_k, w_v, w_gate, b_gate, norm_g, skip, w_down,
                 res_extras):
    inner = ML_HEADS * ML_DH
    t = u.shape[0]
    up = _linear(u, w_up, out_dtype=F32, name="ml_up")
    xc = _dwconv(up, conv_w, geo, n_ch=inner, post=_silu)
    q = _head_linear(xc, w_q, scale=ML_DH ** -0.5)
    k = _head_linear(xc, w_k)
    v = _head_linear(up, w_v)
    wg = jnp.transpose(w_gate, (1, 0, 2)).reshape(3 * inner, 4 * ML_HEADS)
    wg = jnp.zeros((3 * inner, LANES), F32).at[:, :4 * ML_HEADS].set(wg)
    bg = jnp.zeros((1, LANES), F32).at[0, :4 * ML_HEADS].set(b_gate.reshape(-1))
    tm = _pick(t, (1024, 512, 256))
    (pre,) = _mm(
        [q, k, v], [wg, wg, wg], [(0, 0), (1, 1), (2, 2)], grid=(1, t // tm),
        a_specs=[pl.BlockSpec((tm, inner), lambda j, i: (i, 0))] * 3,
        b_specs=[pl.BlockSpec((inner, LANES), lambda j, i, r=r: (r, 0)) for r in range(3)],
        out_shapes=[jax.ShapeDtypeStruct((t, LANES), F32)],
        out_specs=[pl.BlockSpec((tm, LANES), lambda j, i: (i, 0))],
        epilogue=lambda accs, ex: (accs[0] + accs[1] + accs[2] + ex[0],),
        extras=[bg], extra_specs=[pl.BlockSpec((1, LANES), lambda j, i: (0, 0))], name="ml_gates")
    gcols = pl.pallas_call(
        _gate_prep_kernel, grid=(t // CHUNK,),
        in_specs=[pl.BlockSpec((CHUNK, LANES), lambda i: (i, 0))],
        out_specs=pl.BlockSpec((CHUNK, LANES), lambda i: (i, 0)),
        out_shape=jax.ShapeDtypeStruct((t, LANES), F32), name="mlstm_gate_prep",
        compiler_params=pltpu.CompilerParams(dimension_semantics=("arbitrary",)),
    )(pre)
    grows = gcols.T
    c0, n0, m0 = state
    hp, c_p, n_p, m_p = _mlstm_scan(q, k, v, gcols, grows, ln=geo.l_p, nseq=geo.n_p, row0=0)
    hs, _, _, _ = _mlstm_scan(q, k, v, gcols, grows, ln=geo.l_s, nseq=geo.n_s, row0=_tp(geo),
                              init=(c0, n0[:, :, :, None, :], m0[:, :, :, None, None]))
    ys = []
    for hdir, nrows, row0 in ((hp, _tp(geo), 0), (hs, t - _tp(geo), _tp(geo))):
        tmr = 256
        nb = nrows // tmr
        rb0 = row0 // tmr
        y = pl.pallas_call(
            functools.partial(_rowwise_kernel, nin=6, fn=_mlstm_post_fn),
            grid=(nb,),
            in_specs=[pl.BlockSpec((None, tmr, inner), lambda i: (0, i, 0)),
                      pl.BlockSpec((None, tmr, inner), lambda i: (1, i, 0)),
                      pl.BlockSpec((tmr, inner), lambda i: (rb0 + i, 0)),
                      pl.BlockSpec((tmr, inner), lambda i: (rb0 + i, 1)),
                      pl.BlockSpec((1, inner), lambda i: (0, 0)),
                      pl.BlockSpec((1, inner), lambda i: (0, 0))],
            out_specs=[pl.BlockSpec((tmr, inner), lambda i: (i, 0))],
            out_shape=[jax.ShapeDtypeStruct((nrows, inner), BF16)], name="mlstm_post",
            compiler_params=pltpu.CompilerParams(
                dimension_semantics=("arbitrary",),
                vmem_limit_bytes=_vmem_limit(5 * _nbytes((tmr, inner), F32))),
        )(hdir, hdir, xc, up, norm_g[None, :], skip[None, :])[0]
        ys.append(y)
    y = jnp.concatenate(ys, axis=0)
    x_new = _linear(y, w_down, out_dtype=F32, epilogue=_ep_residual, extras=res_extras,
                    extra_specs_fn=_residual_specs(geo), geo=geo, name="ml_down")
    return x_new, (c_p, n_p, m_p)


def _ln_silu_fn(y, g, b):
    mu = jnp.mean(y, axis=-1, keepdims=True)
    var = jnp.mean(jnp.square(y - mu), axis=-1, keepdims=True)
    return (_silu((y - mu) * lax.rsqrt(var + EPS) * g + b),)


def _conformer_mixer(u, geo, w1, b1, dw, b_dw, ln_g, ln_b, w2, b2, res_extras):
    d = D_MODEL
    b1r = b1[None, :]

    def glu(accs, ex):
        return ((accs[0] + ex[0]) * _sigmoid(accs[1] + ex[1]),)

    def especs(i_of, j_of, tm, tn):
        nj = d // tn
        return [pl.BlockSpec((1, tn), lambda *g: (0, j_of(*g))),
                pl.BlockSpec((1, tn), lambda *g: (0, nj + j_of(*g)))]

    y = _dual_linear(u, w1, w1, epilogue=glu, out_dtype=F32, n_out=d, w1_col=0, w2_col=1,
                     extras=[b1r, b1r], extra_specs_fn=especs, name="cv_glu")
    y = _dwconv(y, dw, geo, n_ch=d, bias=b_dw[None, :])
    (y,) = _rowwise(_ln_silu_fn, geo, rows=[(y, d, 0)], params=[(ln_g[None, :], d, 0), (ln_b[None, :], d, 0)],
                    outs=[(d, BF16)], tm=512, name="cv_ln_silu")
    return _linear(y, w2, out_dtype=F32, epilogue=_ep_residual_bias, extras=list(res_extras) + [b2[None, :]],
                   extra_specs_fn=_residual_bias_specs(geo), geo=geo, name="cv_out")


def _ep_swiglu(accs, ex):
    return (_silu(accs[0]) * accs[1],)


def _dense_ffn(u, geo, wg, wu, wd, layer, res_extras):
    _, d, f = wg.shape
    h = _dual_linear(u, wg.reshape(-1, f), wu.reshape(-1, f), epilogue=_ep_swiglu, out_dtype=BF16, n_out=f,
                     tn=_pick(f, (256, 128)), w_row=layer, name="ffn_up")
    return _linear(h, wd.reshape(-1, d), out_dtype=F32, epilogue=_ep_residual, extras=res_extras,
                   extra_specs_fn=_residual_specs(geo), geo=geo, k=f, w_row=layer, name="ffn_down")


def _moe_down_kernel(h_ref, w_ref, cmb_ref, x_ref, g_ref, o_ref, acc_ref):
    e = pl.program_id(2)

    @pl.when(e == 0)
    def _():
        acc_ref[...] = jnp.zeros_like(acc_ref)

    cmb = cmb_ref[...]
    lane = lax.broadcasted_iota(jnp.int32, cmb.shape, 1)
    ce = jnp.sum(jnp.where(lane == e, cmb, 0.0), axis=1, keepdims=True)
    acc_ref[...] += ce * jnp.dot(h_ref[...], w_ref[...].astype(BF16), preferred_element_type=F32)

    @pl.when(e == N_EXPERTS - 1)
    def _():
        o_ref[...] = x_ref[...] + g_ref[...] * acc_ref[...]


def _moe_ffn(u, combine, geo, wg, wu, wd, layer, res_extras):
    t = u.shape[0]
    _, ne, d, f = wg.shape
    tm = _row_tile(t, geo)
    tn = _pick(f, (512, 256, 128))
    nj = f // tn
    (h,) = _mm(
        [u], [wg, wu], [(0, 0), (0, 1)], grid=(ne, nj, t // tm),
        a_specs=[pl.BlockSpec((tm, d), lambda e, j, i: (i, 0))],
        b_specs=[pl.BlockSpec((None, None, d, tn), lambda e, j, i: (layer, e, 0, j))] * 2,
        out_shapes=[jax.ShapeDtypeStruct((ne, t, f), BF16)],
        out_specs=[pl.BlockSpec((None, tm, tn), lambda e, j, i: (e, i, j))],
        epilogue=_ep_swiglu, name="moe_up")
    x, gate = res_extras
    tno = 512
    cond = _cond_of_block(geo, tm)
    sizes = [_nbytes((tm, f), BF16), _nbytes((f, tno), F32), _nbytes((tm, tno), F32) * 4]
    return pl.pallas_call(
        _moe_down_kernel, grid=(d // tno, t // tm, ne),
        in_specs=[pl.BlockSpec((None, tm, f), lambda j, i, e: (e, i, 0)),
                  pl.BlockSpec((None, None, f, tno), lambda j, i, e: (layer, e, 0, j)),
                  pl.BlockSpec((tm, LANES), lambda j, i, e: (i, 0)),
                  pl.BlockSpec((tm, tno), lambda j, i, e: (i, j)),
                  pl.BlockSpec((None, 1, tno), lambda j, i, e: (cond(i), 0, j))],
        out_specs=pl.BlockSpec((tm, tno), lambda j, i, e: (i, j)),
        out_shape=jax.ShapeDtypeStruct((t, d), F32),
        scratch_shapes=[pltpu.VMEM((tm, tno), F32)], name="moe_down",
        compiler_params=pltpu.CompilerParams(
            dimension_semantics=("arbitrary",) * 3, vmem_limit_bytes=_vmem_limit(*sizes)),
    )(h, wd, combine, x, gate)


def _trunk(x, conds, geo, state, p):
    depth = p["w_ada"].shape[0]
    d = D_MODEL
    nco = conds.shape[0]
    cpad = jnp.zeros((8, d), F32).at[:nco].set(conds)
    w_ada = p["w_ada"]
    tn = 512
    nj = 6 * d // tn
    (mod,) = _mm(
        [cpad], [w_ada], [(0, 0)], grid=(depth, nj, 1),
        a_specs=[pl.BlockSpec((8, d), lambda l, j, i: (0, 0))],
        b_specs=[pl.BlockSpec((None, d, tn), lambda l, j, i: (l, 0, j))],
        out_shapes=[jax.ShapeDtypeStruct((depth, 8, 6 * d), F32)],
        out_specs=[pl.BlockSpec((None, 8, tn), lambda l, j, i: (l, 0, j))],
        epilogue=lambda accs, ex: (accs[0] + ex[0],), a_fn=_silu,
        extras=[p["b_ada"][:, None, :]],
        extra_specs=[pl.BlockSpec((None, 1, tn), lambda l, j, i: (l, 0, j))], name="ada_mod")
    mod = mod[:, :nco].reshape(depth, nco, 6, 1, d)

    new_states = []
    for i in range(depth):
        sh1, sc1, g1, sh2, sc2, g2 = (mod[i, :, r] for r in range(6))
        u = _norm_mod(x, p["norm_mix_g"][i][None, :], sc1, sh1, geo)
        kind, j = i % 3, i // 3
        res = [x, g1]
        if kind == 0:
            x = _hyena_mixer(u, geo, p["hy_w_in"][j], p["hy_short"][j], p["hy_filt_w1"][j], p["hy_filt_b1"][j],
                             p["hy_filt_w2"][j], p["hy_filt_b2"][j], p["hy_filt_w3"][j], p["hy_filt_bias"][j],
                             p["hy_w_out"][j], res)
        elif kind == 1:
            st = tuple(s[:, j] for s in state)
            x, new_st = _mlstm_mixer(u, geo, st, p["ml_w_up"][j], p["ml_conv"][j], p["ml_w_q"][j],
                                     p["ml_w_k"][j], p["ml_w_v"][j], p["ml_w_gate"][j], p["ml_b_gate"][j],
                                     p["ml_norm_g"][j], p["ml_skip"][j], p["ml_w_down"][j], res)
            new_states.append(new_st)
        else:
            x = _conformer_mixer(u, geo, p["cv_w1"][j], p["cv_b1"][j], p["cv_dw"][j], p["cv_b_dw"][j],
                                 p["cv_ln_g"][j], p["cv_ln_b"][j], p["cv_w2"][j], p["cv_b2"][j], res)
        jf = i // 2
        g_ffn = p["norm_ffn_g"][i][None, :]
        if i % 2 == 0:
            u = _norm_mod(x, g_ffn, sc2, sh2, geo)
            x = _dense_ffn(u, geo, p["ff_w_gate"], p["ff_w_up"], p["ff_w_down"], jf, [x, g2])
        else:
            u, combine = _norm_mod_route(x, g_ffn, sc2, sh2, p["moe_router"][jf], geo)
            x = _moe_ffn(u, combine, geo, p["moe_w_gate"], p["moe_w_up"], p["moe_w_down"], jf, [x, g2])
    fg = p["final_norm_g"][None, :]
    outs = []
    for nrows, row0 in ((_tp(geo), 0), (_tt(geo) - _tp(geo), _tp(geo))):
        tm = 512
        (y,) = _rowwise(lambda xb, g: (_rms(xb, g),), geo, rows=[(x, d, 0)], params=[(fg, d, 0)],
                        outs=[(d, F32)], tm=tm, n_rows=nrows, row_block0=row0 // tm, name="final_norm")
        outs.append(y)
    new_state = tuple(jnp.stack(parts, axis=1) for parts in zip(*new_states))
    return outs[0], outs[1], new_state


def kernel(x_prompt, x_sample, state_mlstm_C, state_mlstm_n, state_mlstm_m, c, c_ctx, w_ada, b_ada, norm_mix_g, norm_ffn_g, hy_w_in, hy_short, hy_filt_w1, hy_filt_b1, hy_filt_w2, hy_filt_b2, hy_filt_w3, hy_filt_bias, hy_w_out, ml_w_up, ml_conv, ml_w_q, ml_w_k, ml_w_v, ml_w_gate, ml_b_gate, ml_norm_g, ml_skip, ml_w_down, cv_w1, cv_b1, cv_dw, cv_b_dw, cv_ln_g, cv_ln_b, cv_w2, cv_b2, ff_w_gate, ff_w_up, ff_w_down, moe_router, moe_w_gate, moe_w_up, moe_w_down, final_norm_g):
    p = dict(w_ada=w_ada, b_ada=b_ada, norm_mix_g=norm_mix_g, norm_ffn_g=norm_ffn_g, hy_w_in=hy_w_in,
             hy_short=hy_short, hy_filt_w1=hy_filt_w1, hy_filt_b1=hy_filt_b1, hy_filt_w2=hy_filt_w2,
             hy_filt_b2=hy_filt_b2, hy_filt_w3=hy_filt_w3, hy_filt_bias=hy_filt_bias, hy_w_out=hy_w_out,
             ml_w_up=ml_w_up, ml_conv=ml_conv, ml_w_q=ml_w_q, ml_w_k=ml_w_k, ml_w_v=ml_w_v,
             ml_w_gate=ml_w_gate, ml_b_gate=ml_b_gate, ml_norm_g=ml_norm_g, ml_skip=ml_skip,
             ml_w_down=ml_w_down, cv_w1=cv_w1, cv_b1=cv_b1, cv_dw=cv_dw, cv_b_dw=cv_b_dw, cv_ln_g=cv_ln_g,
             cv_ln_b=cv_ln_b, cv_w2=cv_w2, cv_b2=cv_b2, ff_w_gate=ff_w_gate, ff_w_up=ff_w_up,
             ff_w_down=ff_w_down, moe_router=moe_router, moe_w_gate=moe_w_gate, moe_w_up=moe_w_up,
             moe_w_down=moe_w_down, final_norm_g=final_norm_g)
    n_p, l_p, d = x_prompt.shape
    n_s, l_s, _ = x_sample.shape
    geo = Geo(n_p, l_p, n_s, l_s)
    x = jnp.concatenate([x_prompt.reshape(n_p * l_p, d), x_sample.reshape(n_s * l_s, d)], axis=0)
    conds = jnp.concatenate([c_ctx[None, :], c], axis=0)
    y_p, y_s, (c_new, n_new, m_new) = _trunk(
        x, conds, geo, (state_mlstm_C, state_mlstm_n, state_mlstm_m), p)
    n_ml = state_mlstm_C.shape[1]
    return (y_p.reshape(n_p, l_p, d), y_s.reshape(n_s, l_s, d),
            c_new.reshape(n_p, n_ml, 2, ML_HEADS, ML_DH, ML_DH),
            n_new.reshape(n_p, n_ml, 2, ML_HEADS, ML_DH),
            m_new.reshape(n_p, n_ml, 2, ML_HEADS))
```

```python
import collections
import functools
import math

import jax
import jax.numpy as jnp
from jax import lax
from jax.experimental import pallas as pl
from jax.experimental.pallas import tpu as pltpu

F32 = jnp.float32
BF16 = jnp.bfloat16
HIGHEST = lax.Precision.HIGHEST

D_MODEL = 1024
HY_BANDS = 16
HY_SLOW_DECAY = -math.log(1e-2) / 1.5
HY_FAST_DECAY = -math.log(1e-2) / 0.3
HY_SHIFT = 0.05
ML_HEADS = 8
ML_DH = 256
N_EXPERTS = 8
EPS = 1e-6

LANES = 128
SUBLANES = 8
CHUNK = 256
CONV_HALO = 16
CONV_ROWS = 64
MOE_TM = 512
VMEM_CAP = 56 * 1024 * 1024

Geo = collections.namedtuple("Geo", "n_p l_p n_s l_s")


def _tp(geo):
    return geo.n_p * geo.l_p


def _tt(geo):
    return geo.n_p * geo.l_p + geo.n_s * geo.l_s


def _cond_of_block(geo, rows):
    tp = _tp(geo)

    def f(i):
        r0 = i * rows
        return jnp.where(r0 < tp, 0, 1 + (r0 - tp) // geo.l_s)

    return f


def _vmem_limit(*nbytes):
    est = 2 * sum(nbytes) + (8 << 20)
    return int(min(max(est, 32 << 20), VMEM_CAP))


def _nbytes(shape, dtype):
    n = 1
    for s in shape:
        if s is not None:
            n *= s
    return n * jnp.dtype(dtype).itemsize


def _silu(x):
    return x * (1.0 / (1.0 + jnp.exp(-x)))


def _sigmoid(x):
    return 1.0 / (1.0 + jnp.exp(-x))


def _mm_kernel(*refs, na, nb, ne, no, pairs, cast_b, epilogue, a_fn, inner_axis):
    a_refs = refs[:na]
    b_refs = refs[na:na + nb]
    e_refs = refs[na + nb:na + nb + ne]
    o_refs = refs[na + nb + ne:na + nb + ne + no]
    scratch = refs[na + nb + ne + no:]
    first = pl.program_id(inner_axis) == 0
    b_src = []
    si = 0
    for j in range(nb):
        if cast_b[j]:
            s_ref = scratch[si]
            si += 1

            @pl.when(first)
            def _(s_ref=s_ref, b_ref=b_refs[j]):
                s_ref[...] = b_ref[...].astype(BF16)

            b_src.append(s_ref)
        else:
            b_src.append(b_refs[j])
    a_vals = {}
    accs = []
    for (i, j) in pairs:
        if i not in a_vals:
            a = a_refs[i][...]
            if a_fn is not None:
                a = a_fn(a)
            a_vals[i] = a.astype(BF16)
        accs.append(jnp.dot(a_vals[i], b_src[j][...], preferred_element_type=F32))
    outs = epilogue(accs, [e[...] for e in e_refs])
    for o_ref, val in zip(o_refs, outs):
        o_ref[...] = val.astype(o_ref.dtype)


def _mm(a_list, b_list, pairs, *, grid, a_specs, b_specs, out_shapes, out_specs,
        epilogue, extras=(), extra_specs=(), a_fn=None, name="mm"):
    cast_b = tuple(b.dtype != BF16 for b in b_list)
    scratch = [pltpu.VMEM(tuple(s for s in spec.block_shape if s is not None), BF16)
               for b, spec, c in zip(b_list, b_specs, cast_b) if c]
    kern = functools.partial(
        _mm_kernel, na=len(a_list), nb=len(b_list), ne=len(extras), no=len(out_shapes),
        pairs=tuple(pairs), cast_b=cast_b, epilogue=epilogue, a_fn=a_fn, inner_axis=len(grid) - 1)
    sizes = [_nbytes(s.block_shape, a.dtype) for a, s in zip(a_list, a_specs)]
    sizes += [_nbytes(s.block_shape, b.dtype) for b, s in zip(b_list, b_specs)]
    sizes += [_nbytes(s.block_shape, e.dtype) for e, s in zip(extras, extra_specs)]
    sizes += [_nbytes(s.block_shape, o.dtype) for o, s in zip(out_shapes, out_specs)]
    sizes += [_nbytes(s.block_shape, BF16) // 2 for s, c in zip(b_specs, cast_b) if c]
    res = pl.pallas_call(
        kern, grid=grid,
        in_specs=list(a_specs) + list(b_specs) + list(extra_specs),
        out_specs=list(out_specs), out_shape=list(out_shapes),
        scratch_shapes=scratch, name=name,
        compiler_params=pltpu.CompilerParams(
            dimension_semantics=("arbitrary",) * len(grid), vmem_limit_bytes=_vmem_limit(*sizes)),
    )(*a_list, *b_list, *extras)
    return res


def _pick(n, prefs):
    for p in prefs:
        if n % p == 0:
            return p
    return n


def _row_tile(m, geo):
    base = math.gcd(m, geo.l_s, _tp(geo) or m) if geo is not None else m
    return _pick(base, (1024, 512, 256, 128, 8))


def _linear(a, w, *, out_dtype, epilogue=None, extras=(), extra_specs_fn=None, tm=None, tn=None,
            a_col=0, w_col=0, w_row=0, n_out=None, a_fn=None, k=None, geo=None, name="linear"):
    m = a.shape[0]
    k = k or w.shape[0]
    n_out = n_out or w.shape[1]
    tm = tm or _row_tile(m, geo)
    tn = tn or _pick(n_out, (512, 256, 128))
    grid = (n_out // tn, m // tm)
    jo = w_col * (n_out // tn)
    especs = extra_specs_fn(lambda j, i: i, lambda j, i: j, tm, tn) if extras else ()
    ep = epilogue or (lambda accs, ex: (accs[0],))
    (out,) = _mm(
        [a], [w], [(0, 0)], grid=grid,
        a_specs=[pl.BlockSpec((tm, k), lambda j, i: (i, a_col))],
        b_specs=[pl.BlockSpec((k, tn), lambda j, i: (w_row, jo + j))],
        out_shapes=[jax.ShapeDtypeStruct((m, n_out), out_dtype)],
        out_specs=[pl.BlockSpec((tm, tn), lambda j, i: (i, j))],
        epilogue=ep, extras=extras, extra_specs=especs, a_fn=a_fn, name=name)
    return out


def _residual_specs(geo):
    def fn(i_of, j_of, tm, tn):
        cond = _cond_of_block(geo, tm)
        return [pl.BlockSpec((tm, tn), lambda *g: (i_of(*g), j_of(*g))),
                pl.BlockSpec((None, 1, tn), lambda *g: (cond(i_of(*g)), 0, j_of(*g)))]
    return fn


def _residual_bias_specs(geo):
    base = _residual_specs(geo)

    def fn(i_of, j_of, tm, tn):
        return base(i_of, j_of, tm, tn) + [pl.BlockSpec((1, tn), lambda *g: (0, j_of(*g)))]
    return fn


def _ep_residual(accs, ex):
    return (ex[0] + ex[1] * accs[0],)


def _ep_residual_bias(accs, ex):
    return (ex[0] + ex[1] * (accs[0] + ex[2]),)


def _dual_linear(a, w1, w2, *, epilogue, out_dtype, n_out, w1_col=0, w2_col=0, extras=(),
                 extra_specs_fn=None, tm=None, tn=None, geo=None, w_row=0, name="dual_linear"):
    m, k = a.shape
    tm = tm or _row_tile(m, geo)
    tn = tn or _pick(n_out, (512, 256, 128))
    nj = n_out // tn
    grid = (nj, m // tm)
    especs = extra_specs_fn(lambda j, i: i, lambda j, i: j, tm, tn) if extras else ()
    (out,) = _mm(
        [a], [w1, w2], [(0, 0), (0, 1)], grid=grid,
        a_specs=[pl.BlockSpec((tm, k), lambda j, i: (i, 0))],
        b_specs=[pl.BlockSpec((k, tn), lambda j, i: (w_row, w1_col * nj + j)),
                 pl.BlockSpec((k, tn), lambda j, i: (w_row, w2_col * nj + j))],
        out_shapes=[jax.ShapeDtypeStruct((m, n_out), out_dtype)],
        out_specs=[pl.BlockSpec((tm, tn), lambda j, i: (i, j))],
        epilogue=epilogue, extras=extras, extra_specs=especs, name=name)
    return out


def _rowwise_kernel(*refs, nin, fn):
    ins = [r[...] for r in refs[:nin]]
    outs = fn(*ins)
    for o_ref, val in zip(refs[nin:], outs):
        o_ref[...] = val.astype(o_ref.dtype)


def _rowwise(fn, geo, *, rows=(), params=(), conds=(), outs, tm, n_rows=None, row_block0=0, name="rowwise"):
    n_rows = n_rows or rows[0][0].shape[0]
    cond = _cond_of_block(geo, tm)
    in_specs, args, sizes = [], [], []
    for arr, w, cb in rows:
        in_specs.append(pl.BlockSpec((tm, w), lambda i, cb=cb: (i + row_block0, cb)))
        args.append(arr)
        sizes.append(_nbytes((tm, w), arr.dtype))
    for arr, w, cb in params:
        in_specs.append(pl.BlockSpec((1, w), lambda i, cb=cb: (0, cb)))
        args.append(arr)
    for arr in conds:
        w = arr.shape[-1]
        in_specs.append(pl.BlockSpec((None, 1, w), lambda i: (cond(i + row_block0), 0, 0)))
        args.append(arr)
    out_shapes = [jax.ShapeDtypeStruct((n_rows, w), dt) for w, dt in outs]
    out_specs = [pl.BlockSpec((tm, w), lambda i: (i, 0)) for w, dt in outs]
    sizes += [_nbytes((tm, w), dt) for w, dt in outs]
    return pl.pallas_call(
        functools.partial(_rowwise_kernel, nin=len(args), fn=fn),
        grid=(n_rows // tm,), in_specs=in_specs, out_specs=out_specs, out_shape=out_shapes, name=name,
        compiler_params=pltpu.CompilerParams(
            dimension_semantics=("arbitrary",), vmem_limit_bytes=_vmem_limit(*sizes, *sizes)),
    )(*args)


def _rms(x, g):
    return x * lax.rsqrt(jnp.mean(x * x, axis=-1, keepdims=True) + EPS) * g


def _norm_mod_fn(x, g, sc, sh):
    return (_rms(x, g) * (1.0 + sc) + sh,)


def _norm_mod(x, g, sc, sh, geo):
    (u,) = _rowwise(_norm_mod_fn, geo, rows=[(x, D_MODEL, 0)], params=[(g, D_MODEL, 0)],
                    conds=[sc, sh], outs=[(D_MODEL, BF16)], tm=512, name="norm_mod")
    return u


def _norm_mod_route_fn(x, g, router, sc, sh):
    u = _rms(x, g) * (1.0 + sc) + sh
    logits = jnp.dot(u, router, preferred_element_type=F32, precision=HIGHEST)
    lane = lax.broadcasted_iota(jnp.int32, logits.shape, 1)
    neg = jnp.float32(-jnp.inf)
    logits = jnp.where(lane < N_EXPERTS, logits, neg)
    v1 = jnp.max(logits, axis=-1, keepdims=True)
    i1 = jnp.min(jnp.where(logits == v1, lane, LANES), axis=-1, keepdims=True)
    rest = jnp.where(lane == i1, neg, logits)
    v2 = jnp.max(rest, axis=-1, keepdims=True)
    i2 = jnp.min(jnp.where(rest == v2, lane, LANES), axis=-1, keepdims=True)
    e2 = jnp.exp(v2 - v1)
    p1 = 1.0 / (1.0 + e2)
    p2 = e2 / (1.0 + e2)
    info = jnp.where(lane == 0, i1.astype(F32),
                     jnp.where(lane == 1, i2.astype(F32),
                               jnp.where(lane == 2, p1, jnp.where(lane == 3, p2, 0.0))))
    return u, info


def _norm_mod_route(x, g, sc, sh, router, geo):
    tm = 256
    cond = _cond_of_block(geo, tm)
    n = x.shape[0]
    router_p = jnp.zeros((D_MODEL, LANES), F32).at[:, :N_EXPERTS].set(router)
    row = pl.BlockSpec((tm, D_MODEL), lambda i: (i, 0))
    cspec = pl.BlockSpec((None, 1, D_MODEL), lambda i: (cond(i), 0, 0))
    return pl.pallas_call(
        functools.partial(_rowwise_kernel, nin=5, fn=_norm_mod_route_fn),
        grid=(n // tm,),
        in_specs=[row, pl.BlockSpec((1, D_MODEL), lambda i: (0, 0)),
                  pl.BlockSpec((D_MODEL, LANES), lambda i: (0, 0)), cspec, cspec],
        out_specs=[row, pl.BlockSpec((tm, LANES), lambda i: (i, 0))],
        out_shape=[jax.ShapeDtypeStruct((n, D_MODEL), F32), jax.ShapeDtypeStruct((n, LANES), F32)],
        name="norm_mod_route", compiler_params=pltpu.CompilerParams(dimension_semantics=("arbitrary",)),
    )(x, g, router_p, sc, sh)


def _dwconv_kernel(x_ref, w_ref, *rest, taps, geo, rb, post, has_bias):
    if has_bias:
        b_ref, o_ref, pad_ref = rest
    else:
        o_ref, pad_ref = rest
    half = (taps - 1) // 2
    tc = x_ref.shape[1]
    n_prompt_blocks = _tp(geo) // rb
    first = CONV_HALO - half
    win0 = first // SUBLANES * SUBLANES
    shift0 = first - win0
    win_rows = -(-(shift0 + taps - 1 + CONV_ROWS) // SUBLANES) * SUBLANES

    def run(nseq, ln):
        for s in range(nseq):
            base = s * ln
            zeros = jnp.zeros((CONV_HALO, tc), F32)
            pad_ref[0:CONV_HALO, :] = zeros
            pad_ref[CONV_HALO + ln:2 * CONV_HALO + ln, :] = zeros
            pad_ref[CONV_HALO:CONV_HALO + ln, :] = x_ref[base:base + ln, :]
            for r0 in range(0, ln, CONV_ROWS):
                acc = jnp.zeros((CONV_ROWS, tc), F32)
                win = pad_ref[win0 + r0:win0 + r0 + win_rows, :]
                for r in range(SUBLANES):
                    group = [kk for kk in range(taps) if (shift0 + kk) % SUBLANES == r]
                    if not group:
                        continue
                    rot = win if r == 0 else pltpu.roll(win, win_rows - r, 0)
                    for kk in group:
                        m0 = (shift0 + kk) // SUBLANES * SUBLANES
                        acc = acc + w_ref[kk:kk + 1, :] * rot[m0:m0 + CONV_ROWS]
                if has_bias:
                    acc = acc + b_ref[...]
                o_ref[base + r0:base + r0 + CONV_ROWS, :] = post(acc)

    i = pl.program_id(0)
    if n_prompt_blocks > 0:
        @pl.when(i < n_prompt_blocks)
        def _():
            run(rb // geo.l_p, geo.l_p)

    @pl.when(i >= n_prompt_blocks)
    def _():
        run(1, geo.l_s)


def _dwconv(x, w, geo, *, n_ch, bias=None, post=None, tc=256):
    taps = w.shape[0]
    rb = geo.l_s
    t = x.shape[0]
    post = post or (lambda v: v)
    in_specs = [pl.BlockSpec((rb, tc), lambda i, j: (i, j)), pl.BlockSpec((taps, tc), lambda i, j: (0, j))]
    args = [x, w]
    if bias is not None:
        in_specs.append(pl.BlockSpec((1, tc), lambda i, j: (0, j)))
        args.append(bias)
    blk = _nbytes((rb, tc), F32)
    return pl.pallas_call(
        functools.partial(_dwconv_kernel, taps=taps, geo=geo, rb=rb, post=post, has_bias=bias is not None),
        grid=(t // rb, n_ch // tc), in_specs=in_specs,
        out_specs=pl.BlockSpec((rb, tc), lambda i, j: (i, j)),
        out_shape=jax.ShapeDtypeStruct((t, n_ch), F32),
        scratch_shapes=[pltpu.VMEM((rb + 2 * CONV_HALO, tc), F32)], name="dwconv%d" % taps,
        compiler_params=pltpu.CompilerParams(
            dimension_semantics=("arbitrary", "arbitrary"), vmem_limit_bytes=_vmem_limit(blk, blk, blk)),
    )(*args)


def _dft_tables(ln):
    kb = 64
    t3 = jnp.arange(ln, dtype=jnp.int32)[None, None, :]
    a3 = jnp.arange(ln // kb, dtype=jnp.int32)[:, None, None]
    b3 = jnp.arange(kb, dtype=jnp.int32)[None, :, None]
    ang_a = ((kb * a3 * t3) % (2 * ln)).astype(F32) * (math.pi / ln)
    ang_b = ((b3 * t3) % (2 * ln)).astype(F32) * (math.pi / ln)
    ca, sa, cb, sb = jnp.cos(ang_a), jnp.sin(ang_a), jnp.cos(ang_b), jnp.sin(ang_b)
    cos = (ca * cb - sa * sb).reshape(ln, ln)
    sin = (sa * cb + ca * sb).reshape(ln, ln)
    k = jnp.arange(ln, dtype=jnp.int32)[:, None]
    t = jnp.arange(ln, dtype=jnp.int32)[None, :]
    alt_t = jnp.where(t % 2 == 0, 1.0, -1.0).astype(F32)
    fwd_s = jnp.where(k == 0, alt_t, -sin)
    fwd = jnp.concatenate([cos, fwd_s], axis=0).astype(BF16)
    scale = 1.0 / (2 * ln)
    inv_c = jnp.where(k == 0, 1.0, 2.0 * cos).T * scale
    inv_s = jnp.where(k == 0, alt_t, -2.0 * sin).T * scale
    return fwd, inv_c.astype(BF16), inv_s.astype(BF16)


def _filter_hidden_kernel(w1_ref, b1_ref, w2_ref, b2_ref, o_ref, *, ln):
    pos = lax.broadcasted_iota(jnp.int32, (ln, LANES), 0).astype(F32) / ln
    lane = lax.broadcasted_iota(jnp.int32, (ln, LANES), 1)
    band = jnp.where(lane <= HY_BANDS, lane, lane - HY_BANDS).astype(F32)
    ang = (2.0 * math.pi) * pos * band
    feats = jnp.where(lane == 0, pos,
                      jnp.where(lane <= HY_BANDS, jnp.cos(ang),
                                jnp.where(lane <= 2 * HY_BANDS, jnp.sin(ang), 0.0)))
    h = jnp.sin(jnp.dot(feats, w1_ref[...], preferred_element_type=F32, precision=HIGHEST) + b1_ref[...])
    o_ref[...] = jnp.sin(jnp.dot(h, w2_ref[...], preferred_element_type=F32, precision=HIGHEST) + b2_ref[...])


def _filter_kernel(h_ref, w3_ref, hw_ref, ss_ref, *, ln):
    tn = hw_ref.shape[1]
    j = pl.program_id(0)
    h = jnp.dot(h_ref[...], w3_ref[...], preferred_element_type=F32, precision=HIGHEST)
    ch = (j * tn) % D_MODEL + lax.broadcasted_iota(jnp.int32, (1, tn), 1)
    step = (HY_FAST_DECAY - HY_SLOW_DECAY) / (D_MODEL - 1)
    deltas = HY_SLOW_DECAY + ch.astype(F32) * step
    tcol = lax.broadcasted_iota(jnp.int32, (ln, 1), 0).astype(F32) / ln
    h = h * (jnp.exp(-tcol * deltas) + HY_SHIFT)
    hw_ref[...] = h
    ss_ref[...] = jnp.sum(h * h, axis=0, keepdims=True)


def _hyena_filters(ln, w1, b1, w2, b2, w3, fwd):
    hid = w1.shape[1]
    ncol = w3.shape[1]
    tn = 512
    w1p = jnp.zeros((LANES, hid), F32).at[:w1.shape[0]].set(w1)
    full = lambda shape: pl.BlockSpec(shape, lambda j: (0, 0))
    hidden = pl.pallas_call(
        functools.partial(_filter_hidden_kernel, ln=ln),
        grid=(1,),
        in_specs=[full((LANES, hid)), full((1, hid)), full((hid, hid)), full((1, hid))],
        out_specs=full((ln, hid)),
        out_shape=jax.ShapeDtypeStruct((ln, hid), F32),
        name="hy_filter_hidden",
        compiler_params=pltpu.CompilerParams(dimension_semantics=("arbitrary",)),
    )(w1p, b1[None, :], w2, b2[None, :])
    hw, ss = pl.pallas_call(
        functools.partial(_filter_kernel, ln=ln),
        grid=(ncol // tn,),
        in_specs=[full((ln, hid)), pl.BlockSpec((hid, tn), lambda j: (0, j))],
        out_specs=[pl.BlockSpec((ln, tn), lambda j: (0, j)), pl.BlockSpec((1, tn), lambda j: (0, j))],
        out_shape=[jax.ShapeDtypeStruct((ln, ncol), F32), jax.ShapeDtypeStruct((1, ncol), F32)],
        name="hy_filter_cols",
        compiler_params=pltpu.CompilerParams(dimension_semantics=("arbitrary",),
                                             vmem_limit_bytes=_vmem_limit(_nbytes((ln, tn), F32) * 4)),
    )(hidden, w3)
    spec = _linear(fwd, hw, out_dtype=F32, tm=_pick(2 * ln, (1024, 512)), name="hy_filter_dft")
    half = ncol // 2
    tk = _pick(ln, (512, 256))

    def assemble(fc, fs, bc, bs, ssf, ssb):
        scale = lax.rsqrt(ssf + ssb + EPS)
        row0 = (pl.program_id(0) == 0) & (lax.broadcasted_iota(jnp.int32, fc.shape, 0) == 0)
        gc = (fc + bc) * scale
        gs = jnp.where(row0, fs + bs, fs - bs) * scale
        return gc, gs

    nk = ln // tk
    nj = half // tn
    return pl.pallas_call(
        functools.partial(_rowwise_kernel, nin=6, fn=assemble),
        grid=(nk, nj),
        in_specs=[pl.BlockSpec((tk, tn), lambda i, j: (i, j)),
                  pl.BlockSpec((tk, tn), lambda i, j: (nk + i, j)),
                  pl.BlockSpec((tk, tn), lambda i, j: (i, nj + j)),
                  pl.BlockSpec((tk, tn), lambda i, j: (nk + i, nj + j)),
                  pl.BlockSpec((1, tn), lambda i, j: (0, j)),
                  pl.BlockSpec((1, tn), lambda i, j: (0, nj + j))],
        out_specs=[pl.BlockSpec((tk, tn), lambda i, j: (i, j)), pl.BlockSpec((tk, tn), lambda i, j: (i, j))],
        out_shape=[jax.ShapeDtypeStruct((ln, half), F32)] * 2, name="hy_filter_assemble",
        compiler_params=pltpu.CompilerParams(dimension_semantics=("arbitrary", "arbitrary")),
    )(spec, spec, spec, spec, ss, ss)


def _dft_fwd_kernel(fc_ref, fs_ref, z_ref, gc_ref, gs_ref, yc_ref, ys_ref, zb_ref, *, nb):
    @pl.when(pl.program_id(2) == 0)
    def _():
        zb_ref[...] = z_ref[...].astype(BF16)

    fc = fc_ref[...]
    fs = fs_ref[...]
    gcb = gc_ref[...]
    gsb = gs_ref[...]
    row_0 = (pl.program_id(2) == 0) & (lax.broadcasted_iota(jnp.int32, gcb.shape, 0) == 0)
    for s in range(nb):
        zc = jnp.dot(fc, zb_ref[s], preferred_element_type=F32)
        zs = jnp.dot(fs, zb_ref[s], preferred_element_type=F32)
        yc_ref[s] = jnp.where(row_0, zc * gcb, zc * gcb - zs * gsb).astype(yc_ref.dtype)
        ys_ref[s] = jnp.where(row_0, zs * gsb, zc * gsb + zs * gcb).astype(ys_ref.dtype)


def _dft_inv_kernel(ic_ref, is_ref, yc_ref, ys_ref, zin_ref, gate_ref, bias_ref, o_ref, *, nb):
    ic = ic_ref[...]
    isn = is_ref[...]
    bias = bias_ref[...]
    for s in range(nb):
        acc = jnp.dot(ic, yc_ref[s], preferred_element_type=F32)
        acc = acc + jnp.dot(isn, ys_ref[s], preferred_element_type=F32)
        o_ref[s] = (gate_ref[s] * (acc + zin_ref[s] * bias)).astype(o_ref.dtype)


def _long_conv_gated(z, z_col, pc, gate_col, gc, gs, order, bias, tables, *, ln, nseq, row0, out_dtype):
    fwd, inv_c, inv_s = tables
    d = D_MODEL
    tn = 512
    nj = d // tn
    tk = _pick(ln, (512, 256))
    nk = ln // tk
    nb = max(n for n in (8, 4, 2, 1) if nseq % n == 0 and n * ln <= 2048)
    z3 = z.reshape(z.shape[0] // ln, ln, z.shape[1])
    pc3 = pc.reshape(pc.shape[0] // ln, ln, pc.shape[1])
    z_sb0 = (row0 // ln if z.shape[0] != nseq * ln else 0) // nb
    pc_sb0 = (row0 // ln) // nb
    seq_blk = (nb, ln, tn)
    out_blk = (nb, tk, tn)
    cparams = pltpu.CompilerParams(
        dimension_semantics=("arbitrary",) * 3,
        vmem_limit_bytes=_vmem_limit(_nbytes(seq_blk, F32) * 2, _nbytes(out_blk, F32) * 3, _nbytes((tk, ln), BF16) * 2))
    yc, ys = pl.pallas_call(
        functools.partial(_dft_fwd_kernel, nb=nb), grid=(nseq // nb, nj, nk),
        in_specs=[pl.BlockSpec((tk, ln), lambda b, j, i: (i, 0)),
                  pl.BlockSpec((tk, ln), lambda b, j, i: (nk + i, 0)),
                  pl.BlockSpec(seq_blk, lambda b, j, i: (z_sb0 + b, 0, z_col * nj + j)),
                  pl.BlockSpec((tk, tn), lambda b, j, i: (i, order * nj + j)),
                  pl.BlockSpec((tk, tn), lambda b, j, i: (i, order * nj + j))],
        out_specs=[pl.BlockSpec(out_blk, lambda b, j, i: (b, i, j))] * 2,
        out_shape=[jax.ShapeDtypeStruct((nseq, ln, d), BF16)] * 2,
        scratch_shapes=[pltpu.VMEM(seq_blk, BF16)], name="hy_dft_fwd", compiler_params=cparams,
    )(fwd, fwd, z3, gc, gs)
    out = pl.pallas_call(
        functools.partial(_dft_inv_kernel, nb=nb), grid=(nseq // nb, nj, nk),
        in_specs=[pl.BlockSpec((tk, ln), lambda b, j, i: (i, 0)),
                  pl.BlockSpec((tk, ln), lambda b, j, i: (i, 0)),
                  pl.BlockSpec(seq_blk, lambda b, j, i: (b, 0, j)),
                  pl.BlockSpec(seq_blk, lambda b, j, i: (b, 0, j)),
                  pl.BlockSpec(out_blk, lambda b, j, i: (z_sb0 + b, i, z_col * nj + j)),
                  pl.BlockSpec(out_blk, lambda b, j, i: (pc_sb0 + b, i, gate_col * nj + j)),
                  pl.BlockSpec((None, 1, tn), lambda b, j, i: (order, 0, j))],
        out_specs=pl.BlockSpec(out_blk, lambda b, j, i: (b, i, j)),
        out_shape=jax.ShapeDtypeStruct((nseq, ln, d), out_dtype),
        name="hy_dft_inv", compiler_params=cparams,
    )(inv_c, inv_s, yc, ys, z3, pc3, bias[:, None, :])
    return out.reshape(nseq * ln, d)


def _hyena_mixer(u, geo, w_in, short, w1, b1, w2, b2, w3, filt_bias, w_out, res_extras):
    d = D_MODEL
    p = _linear(u, w_in, out_dtype=F32, name="hy_in")
    pc = _dwconv(p, short, geo, n_ch=3 * d)
    parts = []
    for ln, nseq, row0 in ((geo.l_p, geo.n_p, 0), (geo.l_s, geo.n_s, _tp(geo))):
        if nseq == 0:
            continue
        tables = _dft_tables(ln)
        gc, gs = _hyena_filters(ln, w1, b1, w2, b2, w3, tables[0])
        z1 = _long_conv_gated(pc, 0, pc, 1, gc, gs, 0, filt_bias, tables, ln=ln, nseq=nseq, row0=row0,
                              out_dtype=F32)
        z2 = _long_conv_gated(z1, 0, pc, 2, gc, gs, 1, filt_bias, tables, ln=ln, nseq=nseq, row0=row0,
                              out_dtype=BF16)
        parts.append(z2)
    z = jnp.concatenate(parts, axis=0) if len(parts) > 1 else parts[0]
    return _linear(z, w_out, out_dtype=F32, epilogue=_ep_residual, extras=res_extras,
                   extra_specs_fn=_residual_specs(geo), geo=geo, name="hy_out")


def _head_linear(a, ws, scales):
    t = a.shape[0]
    tm = _pick(t, (1024, 512, 256))
    nw = len(ws)
    return _mm(
        [a], [w.reshape(ML_HEADS * ML_DH, ML_DH) for w in ws], [(0, n) for n in range(nw)],
        grid=(ML_HEADS, t // tm),
        a_specs=[pl.BlockSpec((tm, ML_DH), lambda h, i: (i, h))],
        b_specs=[pl.BlockSpec((ML_DH, ML_DH), lambda h, i: (h, 0))] * nw,
        out_shapes=[jax.ShapeDtypeStruct((t, ML_HEADS * ML_DH), BF16)] * nw,
        out_specs=[pl.BlockSpec((tm, ML_DH), lambda h, i: (i, h))] * nw,
        epilogue=lambda accs, ex: tuple(acc if s is None else acc * s for acc, s in zip(accs, scales)),
        name="ml_head_linear")


def _gate_prep_kernel(pre_ref, o_ref):
    pre = pre_ref[...]
    lane = lax.broadcasted_iota(jnp.int32, pre.shape, 1)
    is_f = (lane % 16) >= ML_HEADS
    lf = jnp.minimum(pre, 0.0) - jnp.log(1.0 + jnp.exp(-jnp.abs(pre)))
    r = lax.broadcasted_iota(jnp.int32, (CHUNK, CHUNK), 0)
    c = lax.broadcasted_iota(jnp.int32, (CHUNK, CHUNK), 1)
    lower = (c <= r).astype(F32)
    upper = (c >= r).astype(F32)
    pre_sum = jnp.dot(lower, lf, preferred_element_type=F32, precision=HIGHEST)
    suf_sum = jnp.dot(upper, lf, preferred_element_type=F32, precision=HIGHEST)
    cum = jnp.where(lane < 16, pre_sum, suf_sum)
    o_ref[...] = jnp.where(is_f, cum, pre)


def _mlstm_scan_kernel(*refs, has_init, nc):
    if has_init:
        q_ref, k_ref, v_ref, gc_ref, gr_ref, c0_ref, n0_ref, m0_ref, h_ref, c_ref, n_ref, m_ref = refs
    else:
        q_ref, k_ref, v_ref, gc_ref, gr_ref, h_ref, c_ref, n_ref, m_ref = refs
    dr = pl.program_id(0)
    ci = pl.program_id(2)

    carry = has_init or nc > 1
    if carry:
        @pl.when(ci == 0)
        def _():
            if has_init:
                c_ref[...] = c0_ref[...]
                n_ref[...] = n0_ref[...]
                m_ref[...] = m0_ref[...]
            else:
                c_ref[...] = jnp.zeros_like(c_ref)
                n_ref[...] = jnp.zeros_like(n_ref)
                m_ref[...] = jnp.zeros_like(m_ref)

    gcv = gc_ref[...]
    lane = lax.broadcasted_iota(jnp.int32, gcv.shape, 1)
    rowi = lax.broadcasted_iota(jnp.int32, (CHUNK, CHUNK), 0)
    coli = lax.broadcasted_iota(jnp.int32, (CHUNK, CHUNK), 1)
    mask = (rowi - coli) * (1 - 2 * dr) >= 0
    for hh in range(ML_HEADS):
        cols = slice(hh * ML_DH, (hh + 1) * ML_DH)
        q = q_ref[:, cols]
        k = k_ref[:, cols]
        v = v_ref[:, cols]
        li_idx = dr * 16 + hh
        b_idx = li_idx + ML_HEADS
        bcol = jnp.sum(jnp.where(lane == b_idx, gcv, 0.0), axis=1, keepdims=True)
        licol = jnp.sum(jnp.where(lane == li_idx, gcv, 0.0), axis=1, keepdims=True)
        brow = gr_ref[pl.ds(b_idx, 1), :]
        lirow = gr_ref[pl.ds(li_idx, 1), :]
        m_prev = m_ref[hh] if carry else jnp.zeros((1, 1), F32)

        dmat = jnp.where(mask, bcol - brow + lirow, -jnp.inf)
        a = bcol + m_prev
        mt = jnp.maximum(a, jnp.max(dmat, axis=1, keepdims=True))
        qk = lax.dot_general(q, k, (((1,), (1,)), ((), ())), preferred_element_type=F32)
        s = qk * jnp.exp(dmat - mt)
        num = jnp.dot(s.astype(BF16), v, preferred_element_type=F32)
        den = jnp.sum(s, axis=1, keepdims=True)
        if carry:
            w_inter = jnp.exp(a - mt)
            c_prev = c_ref[hh]
            n_prev = n_ref[hh]
            num = num + w_inter * jnp.dot(q, c_prev.astype(BF16), preferred_element_type=F32)
            den = den + w_inter * jnp.sum(q.astype(F32) * n_prev, axis=1, keepdims=True)
        h_ref[:, cols] = num / jnp.maximum(jnp.abs(den), jnp.exp(-mt))

        b_last = jnp.min(bcol, axis=0, keepdims=True)
        dl = b_last - bcol + licol
        m_new = jnp.maximum(b_last + m_prev, jnp.max(dl, axis=0, keepdims=True))
        kw = k.astype(F32) * jnp.exp(dl - m_new)
        c_new = lax.dot_general(kw.astype(BF16), v, (((0,), (0,)), ((), ())), preferred_element_type=F32)
        n_new = jnp.sum(kw, axis=0, keepdims=True)
        if carry:
            w_old = jnp.exp(b_last + m_prev - m_new)
            c_new = w_old * c_prev + c_new
            n_new = w_old * n_prev + n_new
        c_ref[hh] = c_new
        n_ref[hh] = n_new
        m_ref[hh] = m_new


def _mlstm_scan(q, k, v, gcols, grows, *, ln, nseq, row0, init=None):
    nc = ln // CHUNK
    rb0 = row0 // CHUNK
    hd = ML_HEADS * ML_DH

    def blk(dr, b, c):
        return rb0 + b * nc + c + dr * (nc - 1 - 2 * c)

    row_spec = pl.BlockSpec((CHUNK, hd), lambda dr, b, c: (blk(dr, b, c), 0))
    in_specs = [row_spec, row_spec, row_spec,
                pl.BlockSpec((CHUNK, LANES), lambda dr, b, c: (blk(dr, b, c), 0)),
                pl.BlockSpec((LANES, CHUNK), lambda dr, b, c: (0, blk(dr, b, c)))]
    args = [q, k, v, gcols, grows]
    st_c = pl.BlockSpec((None, None, ML_HEADS, ML_DH, ML_DH), lambda dr, b, c: (b, dr, 0, 0, 0))
    st_n = pl.BlockSpec((None, None, ML_HEADS, 1, ML_DH), lambda dr, b, c: (b, dr, 0, 0, 0))
    st_m = pl.BlockSpec((None, None, ML_HEADS, 1, 1), lambda dr, b, c: (b, dr, 0, 0, 0))
    if init is not None:
        in_specs += [st_c, st_n, st_m]
        args += list(init)
    state_bytes = _nbytes((ML_HEADS, ML_DH, ML_DH), F32)
    return pl.pallas_call(
        functools.partial(_mlstm_scan_kernel, has_init=init is not None, nc=nc),
        grid=(2, nseq, nc), in_specs=in_specs,
        out_specs=[pl.BlockSpec((None, CHUNK, hd),
                                lambda dr, b, c: (dr, b * nc + c + dr * (nc - 1 - 2 * c), 0)),
                   st_c, st_n, st_m],
        out_shape=[jax.ShapeDtypeStruct((2, nseq * ln, hd), F32),
                   jax.ShapeDtypeStruct((nseq, 2, ML_HEADS, ML_DH, ML_DH), F32),
                   jax.ShapeDtypeStruct((nseq, 2, ML_HEADS, 1, ML_DH), F32),
                   jax.ShapeDtypeStruct((nseq, 2, ML_HEADS, 1, 1), F32)],
        name="mlstm_scan_l%d" % ln,
        compiler_params=pltpu.CompilerParams(
            dimension_semantics=("arbitrary",) * 3,
            vmem_limit_bytes=_vmem_limit(3 * _nbytes((CHUNK, hd), BF16), _nbytes((CHUNK, hd), F32),
                                         2 * state_bytes)),
    )(*args)


def _mlstm_post_fn(hf, hb, xc, z, norm_g, skip):
    h = hf + hb
    outs = []
    for hd in range(ML_HEADS):
        sl = slice(hd * ML_DH, (hd + 1) * ML_DH)
        hh = h[:, sl]
        mu = jnp.mean(hh, axis=-1, keepdims=True)
        var = jnp.mean(jnp.square(hh - mu), axis=-1, keepdims=True)
        outs.append((hh - mu) * lax.rsqrt(var + EPS))
    hn = jnp.concatenate(outs, axis=-1)
    y = hn * norm_g + skip * xc
    return (y * _silu(z),)


def _mlstm_mixer(u, geo, state, w_up, conv_w, w_q, w_k, w_v, w_gate, b_gate, norm_g, skip, w_down,
                 res_extras):
    inner = ML_HEADS * ML_DH
    t = u.shape[0]
    up = _linear(u, w_up, out_dtype=F32, name="ml_up")
    xc = _dwconv(up, conv_w, geo, n_ch=inner, post=_silu)
    q, k = _head_linear(xc, [w_q, w_k], [ML_DH ** -0.5, None])
    (v,) = _head_linear(up, [w_v], [None])
    wg = jnp.transpose(w_gate, (1, 0, 2)).reshape(3 * inner, 4 * ML_HEADS)
    wg = jnp.zeros((3 * inner, LANES), F32).at[:, :4 * ML_HEADS].set(wg)
    bg = jnp.zeros((1, LANES), F32).at[0, :4 * ML_HEADS].set(b_gate.reshape(-1))
    tm = _pick(t, (1024, 512, 256))
    (pre,) = _mm(
        [q, k, v], [wg, wg, wg], [(0, 0), (1, 1), (2, 2)], grid=(1, t // tm),
        a_specs=[pl.BlockSpec((tm, inner), lambda j, i: (i, 0))] * 3,
        b_specs=[pl.BlockSpec((inner, LANES), lambda j, i, r=r: (r, 0)) for r in range(3)],
        out_shapes=[jax.ShapeDtypeStruct((t, LANES), F32)],
        out_specs=[pl.BlockSpec((tm, LANES), lambda j, i: (i, 0))],
        epilogue=lambda accs, ex: (accs[0] + accs[1] + accs[2] + ex[0],),
        extras=[bg], extra_specs=[pl.BlockSpec((1, LANES), lambda j, i: (0, 0))], name="ml_gates")
    gcols = pl.pallas_call(
        _gate_prep_kernel, grid=(t // CHUNK,),
        in_specs=[pl.BlockSpec((CHUNK, LANES), lambda i: (i, 0))],
        out_specs=pl.BlockSpec((CHUNK, LANES), lambda i: (i, 0)),
        out_shape=jax.ShapeDtypeStruct((t, LANES), F32), name="mlstm_gate_prep",
        compiler_params=pltpu.CompilerParams(dimension_semantics=("arbitrary",)),
    )(pre)
    grows = gcols.T
    c0, n0, m0 = state
    hp, c_p, n_p, m_p = _mlstm_scan(q, k, v, gcols, grows, ln=geo.l_p, nseq=geo.n_p, row0=0)
    hs, _, _, _ = _mlstm_scan(q, k, v, gcols, grows, ln=geo.l_s, nseq=geo.n_s, row0=_tp(geo),
                              init=(c0, n0[:, :, :, None, :], m0[:, :, :, None, None]))
    ys = []
    for hdir, nrows, row0 in ((hp, _tp(geo), 0), (hs, t - _tp(geo), _tp(geo))):
        tmr = 256
        nb = nrows // tmr
        rb0 = row0 // tmr
        y = pl.pallas_call(
            functools.partial(_rowwise_kernel, nin=6, fn=_mlstm_post_fn),
            grid=(nb,),
            in_specs=[pl.BlockSpec((None, tmr, inner), lambda i: (0, i, 0)),
                      pl.BlockSpec((None, tmr, inner), lambda i: (1, i, 0)),
                      pl.BlockSpec((tmr, inner), lambda i: (rb0 + i, 0)),
                      pl.BlockSpec((tmr, inner), lambda i: (rb0 + i, 1)),
                      pl.BlockSpec((1, inner), lambda i: (0, 0)),
                      pl.BlockSpec((1, inner), lambda i: (0, 0))],
            out_specs=[pl.BlockSpec((tmr, inner), lambda i: (i, 0))],
            out_shape=[jax.ShapeDtypeStruct((nrows, inner), BF16)], name="mlstm_post",
            compiler_params=pltpu.CompilerParams(
                dimension_semantics=("arbitrary",),
                vmem_limit_bytes=_vmem_limit(5 * _nbytes((tmr, inner), F32))),
        )(hdir, hdir, xc, up, norm_g[None, :], skip[None, :])[0]
        ys.append(y)
    y = jnp.concatenate(ys, axis=0)
    x_new = _linear(y, w_down, out_dtype=F32, epilogue=_ep_residual, extras=res_extras,
                    extra_specs_fn=_residual_specs(geo), geo=geo, name="ml_down")
    return x_new, (c_p, n_p, m_p)


def _ln_silu_fn(y, g, b):
    mu = jnp.mean(y, axis=-1, keepdims=True)
    var = jnp.mean(jnp.square(y - mu), axis=-1, keepdims=True)
    return (_silu((y - mu) * lax.rsqrt(var + EPS) * g + b),)


def _conformer_mixer(u, geo, w1, b1, dw, b_dw, ln_g, ln_b, w2, b2, res_extras):
    d = D_MODEL
    b1r = b1[None, :]

    def glu(accs, ex):
        return ((accs[0] + ex[0]) * _sigmoid(accs[1] + ex[1]),)

    def especs(i_of, j_of, tm, tn):
        nj = d // tn
        return [pl.BlockSpec((1, tn), lambda *g: (0, j_of(*g))),
                pl.BlockSpec((1, tn), lambda *g: (0, nj + j_of(*g)))]

    y = _dual_linear(u, w1, w1, epilogue=glu, out_dtype=F32, n_out=d, w1_col=0, w2_col=1,
                     extras=[b1r, b1r], extra_specs_fn=especs, name="cv_glu")
    y = _dwconv(y, dw, geo, n_ch=d, bias=b_dw[None, :])
    (y,) = _rowwise(_ln_silu_fn, geo, rows=[(y, d, 0)], params=[(ln_g[None, :], d, 0), (ln_b[None, :], d, 0)],
                    outs=[(d, BF16)], tm=512, name="cv_ln_silu")
    return _linear(y, w2, out_dtype=F32, epilogue=_ep_residual_bias, extras=list(res_extras) + [b2[None, :]],
                   extra_specs_fn=_residual_bias_specs(geo), geo=geo, name="cv_out")


def _ep_swiglu(accs, ex):
    return (_silu(accs[0]) * accs[1],)


def _dense_ffn(u, geo, wg, wu, wd, layer, res_extras):
    _, d, f = wg.shape
    h = _dual_linear(u, wg.reshape(-1, f), wu.reshape(-1, f), epilogue=_ep_swiglu, out_dtype=BF16, n_out=f,
                     tn=_pick(f, (256, 128)), w_row=layer, name="ffn_up")
    return _linear(h, wd.reshape(-1, d), out_dtype=F32, epilogue=_ep_residual, extras=res_extras,
                   extra_specs_fn=_residual_specs(geo), geo=geo, k=f, w_row=layer, name="ffn_down")


def _lane_col(blk, idx):
    lane = lax.broadcasted_iota(jnp.int32, blk.shape, 1)
    return jnp.sum(jnp.where(lane == idx, blk, 0.0), axis=1, keepdims=True)


def _route_rank_kernel(info_ref, rank_ref, cnt_ref, carry_ref):
    @pl.when(pl.program_id(0) == 0)
    def _():
        carry_ref[...] = jnp.zeros_like(carry_ref)

    info = info_ref[...]
    lane = lax.broadcasted_iota(jnp.int32, info.shape, 1)
    lanef = lane.astype(F32)
    i1 = _lane_col(info, 0)
    i2 = _lane_col(info, 1)
    sel = jnp.where((lanef == i1) | (lanef == i2), 1.0, 0.0)
    r = lax.broadcasted_iota(jnp.int32, (CHUNK, CHUNK), 0)
    c = lax.broadcasted_iota(jnp.int32, (CHUNK, CHUNK), 1)
    strict = (c < r).astype(BF16)
    before = jnp.dot(strict, sel.astype(BF16), preferred_element_type=F32) + carry_ref[...]
    r1 = jnp.sum(jnp.where(lanef == i1, before, 0.0), axis=1, keepdims=True)
    r2 = jnp.sum(jnp.where(lanef == i2, before, 0.0), axis=1, keepdims=True)
    rank_ref[...] = jnp.where(lane == 0, r1, jnp.where(lane == 1, r2, 0.0))
    carry = carry_ref[...] + jnp.sum(sel, axis=0, keepdims=True)
    carry_ref[...] = carry
    cnt_ref[...] = carry


def _moe_gather_kernel(inv_ref, u_hbm, xs_ref, buf, sem):
    tm = buf.shape[0]

    def row_copy(src_row, dst_row):
        return pltpu.make_async_copy(u_hbm.at[pl.ds(src_row, 1), :], buf.at[pl.ds(dst_row, 1), :], sem)

    def issue(r, carry):
        row_copy(inv_ref[0, r], r).start()
        return carry

    def drain(r, carry):
        row_copy(0, r).wait()
        return carry

    lax.fori_loop(0, tm, issue, 0, unroll=8)
    lax.fori_loop(0, tm, drain, 0, unroll=8)
    xs_ref[...] = buf[...].astype(xs_ref.dtype)


def _moe_up_kernel(te_ref, xs_ref, wg_ref, wu_ref, h_ref, wg_b, wu_b, *, nt):
    i = pl.program_id(1)

    @pl.when(i < te_ref[nt])
    def _():
        @pl.when((i == 0) | (te_ref[i] != te_ref[jnp.maximum(i - 1, 0)]))
        def _():
            wg_b[...] = wg_ref[...].astype(BF16)
            wu_b[...] = wu_ref[...].astype(BF16)

        a = xs_ref[...]
        gate = jnp.dot(a, wg_b[...], preferred_element_type=F32)
        up = jnp.dot(a, wu_b[...], preferred_element_type=F32)
        h_ref[...] = (_silu(gate) * up).astype(h_ref.dtype)

    @pl.when(i >= te_ref[nt])
    def _():
        h_ref[...] = jnp.zeros_like(h_ref)


def _moe_down_kernel(te_ref, h_ref, w_ref, y_ref, w_b, *, nt):
    i = pl.program_id(1)

    @pl.when(i < te_ref[nt])
    def _():
        @pl.when((i == 0) | (te_ref[i] != te_ref[jnp.maximum(i - 1, 0)]))
        def _():
            w_b[...] = w_ref[...].astype(BF16)

        y_ref[...] = jnp.dot(h_ref[...], w_b[...], preferred_element_type=F32)

    @pl.when(i >= te_ref[nt])
    def _():
        y_ref[...] = jnp.zeros_like(y_ref)


def _moe_combine_kernel(p1_ref, p2_ref, ys_hbm, x_ref, g_ref, info_ref, o_ref, buf1, buf2, sem):
    tm = buf1.shape[0]

    def row_copy(src_row, buf, dst_row):
        return pltpu.make_async_copy(ys_hbm.at[pl.ds(src_row, 1), :], buf.at[pl.ds(dst_row, 1), :], sem)

    def issue(r, carry):
        row_copy(p1_ref[0, r], buf1, r).start()
        row_copy(p2_ref[0, r], buf2, r).start()
        return carry

    def drain(r, carry):
        row_copy(0, buf1, r).wait()
        row_copy(0, buf2, r).wait()
        return carry

    lax.fori_loop(0, tm, issue, 0, unroll=8)
    lax.fori_loop(0, tm, drain, 0, unroll=8)
    info = info_ref[...]
    y = _lane_col(info, 2) * buf1[...] + _lane_col(info, 3) * buf2[...]
    o_ref[...] = x_ref[...] + g_ref[...] * y


def _moe_ffn(u, info, geo, wg, wu, wd, layer, res_extras):
    t, d = u.shape
    _, ne, _, f = wg.shape
    tm = MOE_TM
    nt = (2 * t) // tm + ne
    rows = nt * tm
    rank, cnt = pl.pallas_call(
        _route_rank_kernel, grid=(t // CHUNK,),
        in_specs=[pl.BlockSpec((CHUNK, LANES), lambda i: (i, 0))],
        out_specs=[pl.BlockSpec((CHUNK, LANES), lambda i: (i, 0)), pl.BlockSpec((1, LANES), lambda i: (0, 0))],
        out_shape=[jax.ShapeDtypeStruct((t, LANES), F32), jax.ShapeDtypeStruct((1, LANES), F32)],
        scratch_shapes=[pltpu.VMEM((1, LANES), F32)], name="moe_rank",
        compiler_params=pltpu.CompilerParams(dimension_semantics=("arbitrary",)),
    )(info)
    counts = cnt[0, :ne].astype(jnp.int32)
    gsz = (counts + tm - 1) // tm * tm
    ends = jnp.cumsum(gsz)
    off = ends - gsz
    e1 = info[:, 0].astype(jnp.int32)
    e2 = info[:, 1].astype(jnp.int32)
    pos1 = off[e1] + rank[:, 0].astype(jnp.int32)
    pos2 = off[e2] + rank[:, 1].astype(jnp.int32)
    tok = jnp.arange(t, dtype=jnp.int32)
    inv = jnp.zeros((rows,), jnp.int32).at[jnp.concatenate([pos1, pos2])].set(jnp.concatenate([tok, tok]))
    tile_expert = jnp.sum(jnp.arange(nt, dtype=jnp.int32)[:, None] * tm >= ends[None, :], axis=1)
    te = jnp.concatenate([jnp.minimum(tile_expert, ne - 1), (ends[-1:] // tm)]).astype(jnp.int32)

    smem_row = lambda n: pl.BlockSpec((None, 1, n), lambda i: (i, 0, 0), memory_space=pltpu.SMEM)
    xs = pl.pallas_call(
        _moe_gather_kernel, grid=(nt,),
        in_specs=[smem_row(tm), pl.BlockSpec(memory_space=pl.ANY)],
        out_specs=pl.BlockSpec((tm, d), lambda i: (i, 0)),
        out_shape=jax.ShapeDtypeStruct((rows, d), BF16),
        scratch_shapes=[pltpu.VMEM((tm, d), F32), pltpu.SemaphoreType.DMA(())], name="moe_gather",
        compiler_params=pltpu.CompilerParams(dimension_semantics=("arbitrary",)),
    )(inv.reshape(nt, 1, tm), u)

    def used(i, te_ref):
        return jnp.minimum(i, te_ref[nt] - 1)

    tn = _pick(f, (512, 256, 128))
    h = pl.pallas_call(
        functools.partial(_moe_up_kernel, nt=nt),
        grid_spec=pltpu.PrefetchScalarGridSpec(
            num_scalar_prefetch=1, grid=(f // tn, nt),
            in_specs=[pl.BlockSpec((tm, d), lambda j, i, te_ref: (used(i, te_ref), 0)),
                      pl.BlockSpec((None, None, d, tn), lambda j, i, te_ref: (layer, te_ref[i], 0, j)),
                      pl.BlockSpec((None, None, d, tn), lambda j, i, te_ref: (layer, te_ref[i], 0, j))],
            out_specs=pl.BlockSpec((tm, tn), lambda j, i, te_ref: (i, j)),
            scratch_shapes=[pltpu.VMEM((d, tn), BF16), pltpu.VMEM((d, tn), BF16)]),
        out_shape=jax.ShapeDtypeStruct((rows, f), BF16), name="moe_up",
        compiler_params=pltpu.CompilerParams(
            dimension_semantics=("arbitrary",) * 2,
            vmem_limit_bytes=_vmem_limit(_nbytes((d, tn), F32) * 2, _nbytes((tm, d), BF16), _nbytes((tm, tn), F32) * 3)),
    )(te, xs, wg, wu)
    tno = 512
    ys = pl.pallas_call(
        functools.partial(_moe_down_kernel, nt=nt),
        grid_spec=pltpu.PrefetchScalarGridSpec(
            num_scalar_prefetch=1, grid=(d // tno, nt),
            in_specs=[pl.BlockSpec((tm, f), lambda j, i, te_ref: (used(i, te_ref), 0)),
                      pl.BlockSpec((None, None, f, tno), lambda j, i, te_ref: (layer, te_ref[i], 0, j))],
            out_specs=pl.BlockSpec((tm, tno), lambda j, i, te_ref: (i, j)),
            scratch_shapes=[pltpu.VMEM((f, tno), BF16)]),
        out_shape=jax.ShapeDtypeStruct((rows, d), F32), name="moe_down",
        compiler_params=pltpu.CompilerParams(
            dimension_semantics=("arbitrary",) * 2,
            vmem_limit_bytes=_vmem_limit(_nbytes((f, tno), F32), _nbytes((f, tno), BF16),
                                         _nbytes((tm, f), BF16), _nbytes((tm, tno), F32))),
    )(te, h, wd)

    x, gate = res_extras
    tc = CHUNK
    cond = _cond_of_block(geo, tc)
    return pl.pallas_call(
        _moe_combine_kernel, grid=(t // tc,),
        in_specs=[smem_row(tc), smem_row(tc), pl.BlockSpec(memory_space=pl.ANY),
                  pl.BlockSpec((tc, d), lambda i: (i, 0)),
                  pl.BlockSpec((None, 1, d), lambda i: (cond(i), 0, 0)),
                  pl.BlockSpec((tc, LANES), lambda i: (i, 0))],
        out_specs=pl.BlockSpec((tc, d), lambda i: (i, 0)),
        out_shape=jax.ShapeDtypeStruct((t, d), F32),
        scratch_shapes=[pltpu.VMEM((tc, d), F32), pltpu.VMEM((tc, d), F32), pltpu.SemaphoreType.DMA(())],
        name="moe_combine",
        compiler_params=pltpu.CompilerParams(dimension_semantics=("arbitrary",)),
    )(pos1.reshape(t // tc, 1, tc), pos2.reshape(t // tc, 1, tc), ys, x, gate, info)


def _trunk(x, conds, geo, state, p):
    depth = p["w_ada"].shape[0]
    d = D_MODEL
    nco = conds.shape[0]
    cpad = jnp.zeros((8, d), F32).at[:nco].set(conds)
    w_ada = p["w_ada"]
    tn = 512
    nj = 6 * d // tn
    (mod,) = _mm(
        [cpad], [w_ada], [(0, 0)], grid=(depth, nj, 1),
        a_specs=[pl.BlockSpec((8, d), lambda l, j, i: (0, 0))],
        b_specs=[pl.BlockSpec((None, d, tn), lambda l, j, i: (l, 0, j))],
        out_shapes=[jax.ShapeDtypeStruct((depth, 8, 6 * d), F32)],
        out_specs=[pl.BlockSpec((None, 8, tn), lambda l, j, i: (l, 0, j))],
        epilogue=lambda accs, ex: (accs[0] + ex[0],), a_fn=_silu,
        extras=[p["b_ada"][:, None, :]],
        extra_specs=[pl.BlockSpec((None, 1, tn), lambda l, j, i: (l, 0, j))], name="ada_mod")
    mod = mod[:, :nco].reshape(depth, nco, 6, 1, d)

    new_states = []
    for i in range(depth):
        sh1, sc1, g1, sh2, sc2, g2 = (mod[i, :, r] for r in range(6))
        u = _norm_mod(x, p["norm_mix_g"][i][None, :], sc1, sh1, geo)
        kind, j = i % 3, i // 3
        res = [x, g1]
        if kind == 0:
            x = _hyena_mixer(u, geo, p["hy_w_in"][j], p["hy_short"][j], p["hy_filt_w1"][j], p["hy_filt_b1"][j],
                             p["hy_filt_w2"][j], p["hy_filt_b2"][j], p["hy_filt_w3"][j], p["hy_filt_bias"][j],
                             p["hy_w_out"][j], res)
        elif kind == 1:
            st = tuple(s[:, j] for s in state)
            x, new_st = _mlstm_mixer(u, geo, st, p["ml_w_up"][j], p["ml_conv"][j], p["ml_w_q"][j],
                                     p["ml_w_k"][j], p["ml_w_v"][j], p["ml_w_gate"][j], p["ml_b_gate"][j],
                                     p["ml_norm_g"][j], p["ml_skip"][j], p["ml_w_down"][j], res)
            new_states.append(new_st)
        else:
            x = _conformer_mixer(u, geo, p["cv_w1"][j], p["cv_b1"][j], p["cv_dw"][j], p["cv_b_dw"][j],
                                 p["cv_ln_g"][j], p["cv_ln_b"][j], p["cv_w2"][j], p["cv_b2"][j], res)
        jf = i // 2
        g_ffn = p["norm_ffn_g"][i][None, :]
        if i % 2 == 0:
            u = _norm_mod(x, g_ffn, sc2, sh2, geo)
            x = _dense_ffn(u, geo, p["ff_w_gate"], p["ff_w_up"], p["ff_w_down"], jf, [x, g2])
        else:
            u, info = _norm_mod_route(x, g_ffn, sc2, sh2, p["moe_router"][jf], geo)
            x = _moe_ffn(u, info, geo, p["moe_w_gate"], p["moe_w_up"], p["moe_w_down"], jf, [x, g2])
    fg = p["final_norm_g"][None, :]
    outs = []
    for nrows, row0 in ((_tp(geo), 0), (_tt(geo) - _tp(geo), _tp(geo))):
        tm = 512
        (y,) = _rowwise(lambda xb, g: (_rms(xb, g),), geo, rows=[(x, d, 0)], params=[(fg, d, 0)],
                        outs=[(d, F32)], tm=tm, n_rows=nrows, row_block0=row0 // tm, name="final_norm")
        outs.append(y)
    new_state = tuple(jnp.stack(parts, axis=1) for parts in zip(*new_states))
    return outs[0], outs[1], new_state


def kernel(x_prompt, x_sample, state_mlstm_C, state_mlstm_n, state_mlstm_m, c, c_ctx, w_ada, b_ada, norm_mix_g, norm_ffn_g, hy_w_in, hy_short, hy_filt_w1, hy_filt_b1, hy_filt_w2, hy_filt_b2, hy_filt_w3, hy_filt_bias, hy_w_out, ml_w_up, ml_conv, ml_w_q, ml_w_k, ml_w_v, ml_w_gate, ml_b_gate, ml_norm_g, ml_skip, ml_w_down, cv_w1, cv_b1, cv_dw, cv_b_dw, cv_ln_g, cv_ln_b, cv_w2, cv_b2, ff_w_gate, ff_w_up, ff_w_down, moe_router, moe_w_gate, moe_w_up, moe_w_down, final_norm_g):
    p = dict(w_ada=w_ada, b_ada=b_ada, norm_mix_g=norm_mix_g, norm_ffn_g=norm_ffn_g, hy_w_in=hy_w_in,
             hy_short=hy_short, hy_filt_w1=hy_filt_w1, hy_filt_b1=hy_filt_b1, hy_filt_w2=hy_filt_w2,
             hy_filt_b2=hy_filt_b2, hy_filt_w3=hy_filt_w3, hy_filt_bias=hy_filt_bias, hy_w_out=hy_w_out,
             ml_w_up=ml_w_up, ml_conv=ml_conv, ml_w_q=ml_w_q, ml_w_k=ml_w_k, ml_w_v=ml_w_v,
             ml_w_gate=ml_w_gate, ml_b_gate=ml_b_gate, ml_norm_g=ml_norm_g, ml_skip=ml_skip,
             ml_w_down=ml_w_down, cv_w1=cv_w1, cv_b1=cv_b1, cv_dw=cv_dw, cv_b_dw=cv_b_dw, cv_ln_g=cv_ln_g,
             cv_ln_b=cv_ln_b, cv_w2=cv_w2, cv_b2=cv_b2, ff_w_gate=ff_w_gate, ff_w_up=ff_w_up,
             ff_w_down=ff_w_down, moe_router=moe_router, moe_w_gate=moe_w_gate, moe_w_up=moe_w_up,
             moe_w_down=moe_w_down, final_norm_g=final_norm_g)
    n_p, l_p, d = x_prompt.shape
    n_s, l_s, _ = x_sample.shape
    geo = Geo(n_p, l_p, n_s, l_s)
    x = jnp.concatenate([x_prompt.reshape(n_p * l_p, d), x_sample.reshape(n_s * l_s, d)], axis=0)
    conds = jnp.concatenate([c_ctx[None, :], c], axis=0)
    y_p, y_s, (c_new, n_new, m_new) = _trunk(
        x, conds, geo, (state_mlstm_C, state_mlstm_n, state_mlstm_m), p)
    n_ml = state_mlstm_C.shape[1]
    return (y_p.reshape(n_p, l_p, d), y_s.reshape(n_s, l_s, d),
            c_new.reshape(n_p, n_ml, 2, ML_HEADS, ML_DH, ML_DH),
            n_new.reshape(n_p, n_ml, 2, ML_HEADS, ML_DH),
            m_new.reshape(n_p, n_ml, 2, ML_HEADS))
```

```python
import collections
import functools
import math

import jax
import jax.numpy as jnp
from jax import lax
from jax.experimental import pallas as pl
from jax.experimental.pallas import tpu as pltpu

F32 = jnp.float32
BF16 = jnp.bfloat16
HIGHEST = lax.Precision.HIGHEST

D_MODEL = 1024
HY_BANDS = 16
HY_SLOW_DECAY = -math.log(1e-2) / 1.5
HY_FAST_DECAY = -math.log(1e-2) / 0.3
HY_SHIFT = 0.05
ML_HEADS = 8
ML_DH = 256
N_EXPERTS = 8
EPS = 1e-6

LANES = 128
SUBLANES = 8
CHUNK = 256
CONV_HALO = 16
CONV_ROWS = 64
MOE_TM = 1024
VMEM_CAP = 56 * 1024 * 1024
COL_TILE_BUDGET = 40 * 1024 * 1024

Geo = collections.namedtuple("Geo", "n_p l_p n_s l_s")


def _tp(geo):
    return geo.n_p * geo.l_p


def _tt(geo):
    return geo.n_p * geo.l_p + geo.n_s * geo.l_s


def _cond_of_block(geo, rows):
    tp = _tp(geo)

    def f(i):
        r0 = i * rows
        return jnp.where(r0 < tp, 0, 1 + (r0 - tp) // geo.l_s)

    return f


def _vmem_limit(*nbytes):
    est = 2 * sum(nbytes) + (8 << 20)
    return int(min(max(est, 32 << 20), VMEM_CAP))


def _nbytes(shape, dtype):
    n = 1
    for s in shape:
        if s is not None:
            n *= s
    return n * jnp.dtype(dtype).itemsize


def _silu(x):
    return x * (1.0 / (1.0 + jnp.exp(-x)))


def _sigmoid(x):
    return 1.0 / (1.0 + jnp.exp(-x))


def _mm_kernel(*refs, na, nb, ne, no, pairs, cast_b, epilogue, a_fn, inner_axis):
    a_refs = refs[:na]
    b_refs = refs[na:na + nb]
    e_refs = refs[na + nb:na + nb + ne]
    o_refs = refs[na + nb + ne:na + nb + ne + no]
    scratch = refs[na + nb + ne + no:]
    first = pl.program_id(inner_axis) == 0
    b_src = []
    si = 0
    for j in range(nb):
        if cast_b[j]:
            s_ref = scratch[si]
            si += 1

            @pl.when(first)
            def _(s_ref=s_ref, b_ref=b_refs[j]):
                s_ref[...] = b_ref[...].astype(BF16)

            b_src.append(s_ref)
        else:
            b_src.append(b_refs[j])
    a_vals = {}
    accs = []
    for (i, j) in pairs:
        if i not in a_vals:
            a = a_refs[i][...]
            if a_fn is not None:
                a = a_fn(a)
            a_vals[i] = a.astype(BF16)
        accs.append(jnp.dot(a_vals[i], b_src[j][...], preferred_element_type=F32))
    outs = epilogue(accs, [e[...] for e in e_refs])
    for o_ref, val in zip(o_refs, outs):
        o_ref[...] = val.astype(o_ref.dtype)


def _mm(a_list, b_list, pairs, *, grid, a_specs, b_specs, out_shapes, out_specs,
        epilogue, extras=(), extra_specs=(), a_fn=None, name="mm"):
    cast_b = tuple(b.dtype != BF16 for b in b_list)
    scratch = [pltpu.VMEM(tuple(s for s in spec.block_shape if s is not None), BF16)
               for b, spec, c in zip(b_list, b_specs, cast_b) if c]
    kern = functools.partial(
        _mm_kernel, na=len(a_list), nb=len(b_list), ne=len(extras), no=len(out_shapes),
        pairs=tuple(pairs), cast_b=cast_b, epilogue=epilogue, a_fn=a_fn, inner_axis=len(grid) - 1)
    sizes = [_nbytes(s.block_shape, a.dtype) for a, s in zip(a_list, a_specs)]
    sizes += [_nbytes(s.block_shape, b.dtype) for b, s in zip(b_list, b_specs)]
    sizes += [_nbytes(s.block_shape, e.dtype) for e, s in zip(extras, extra_specs)]
    sizes += [_nbytes(s.block_shape, o.dtype) for o, s in zip(out_shapes, out_specs)]
    sizes += [_nbytes(s.block_shape, BF16) // 2 for s, c in zip(b_specs, cast_b) if c]
    res = pl.pallas_call(
        kern, grid=grid,
        in_specs=list(a_specs) + list(b_specs) + list(extra_specs),
        out_specs=list(out_specs), out_shape=list(out_shapes),
        scratch_shapes=scratch, name=name,
        compiler_params=pltpu.CompilerParams(
            dimension_semantics=("arbitrary",) * len(grid), vmem_limit_bytes=_vmem_limit(*sizes)),
    )(*a_list, *b_list, *extras)
    return res


def _pick(n, prefs):
    for p in prefs:
        if n % p == 0:
            return p
    return n


def _row_tile(m, geo):
    base = math.gcd(m, geo.l_s, _tp(geo) or m) if geo is not None else m
    return _pick(base, (1024, 512, 256, 128, 8))


def _col_tile(k, n_out, n_w, tm, out_bytes):
    for tn in (1408, 1024, 512, 256):
        need = n_w * k * tn * (2 * 4 + 2) + 2 * tm * k * 2 + 2 * tm * tn * (out_bytes + 4)
        if n_out % tn == 0 and need <= COL_TILE_BUDGET:
            return tn
    return LANES


def _linear(a, w, *, out_dtype, epilogue=None, extras=(), extra_specs_fn=None, tm=None, tn=None,
            a_col=0, w_col=0, w_row=0, n_out=None, a_fn=None, k=None, geo=None, name="linear"):
    m = a.shape[0]
    k = k or w.shape[0]
    n_out = n_out or w.shape[1]
    tm = tm or _row_tile(m, geo)
    tn = tn or _col_tile(k, n_out, 1, tm, jnp.dtype(out_dtype).itemsize)
    grid = (n_out // tn, m // tm)
    jo = w_col * (n_out // tn)
    especs = extra_specs_fn(lambda j, i: i, lambda j, i: j, tm, tn) if extras else ()
    ep = epilogue or (lambda accs, ex: (accs[0],))
    (out,) = _mm(
        [a], [w], [(0, 0)], grid=grid,
        a_specs=[pl.BlockSpec((tm, k), lambda j, i: (i, a_col))],
        b_specs=[pl.BlockSpec((k, tn), lambda j, i: (w_row, jo + j))],
        out_shapes=[jax.ShapeDtypeStruct((m, n_out), out_dtype)],
        out_specs=[pl.BlockSpec((tm, tn), lambda j, i: (i, j))],
        epilogue=ep, extras=extras, extra_specs=especs, a_fn=a_fn, name=name)
    return out


def _residual_specs(geo):
    def fn(i_of, j_of, tm, tn):
        cond = _cond_of_block(geo, tm)
        return [pl.BlockSpec((tm, tn), lambda *g: (i_of(*g), j_of(*g))),
                pl.BlockSpec((None, 1, tn), lambda *g: (cond(i_of(*g)), 0, j_of(*g)))]
    return fn


def _residual_bias_specs(geo):
    base = _residual_specs(geo)

    def fn(i_of, j_of, tm, tn):
        return base(i_of, j_of, tm, tn) + [pl.BlockSpec((1, tn), lambda *g: (0, j_of(*g)))]
    return fn


def _ep_residual(accs, ex):
    return (ex[0] + ex[1] * accs[0],)


def _ep_residual_bias(accs, ex):
    return (ex[0] + ex[1] * (accs[0] + ex[2]),)


def _dual_linear(a, w1, w2, *, epilogue, out_dtype, n_out, w1_col=0, w2_col=0, extras=(),
                 extra_specs_fn=None, tm=None, tn=None, geo=None, w_row=0, name="dual_linear"):
    m, k = a.shape
    tm = tm or _row_tile(m, geo)
    tn = tn or _col_tile(k, n_out, 2, tm, jnp.dtype(out_dtype).itemsize)
    nj = n_out // tn
    grid = (nj, m // tm)
    especs = extra_specs_fn(lambda j, i: i, lambda j, i: j, tm, tn) if extras else ()
    (out,) = _mm(
        [a], [w1, w2], [(0, 0), (0, 1)], grid=grid,
        a_specs=[pl.BlockSpec((tm, k), lambda j, i: (i, 0))],
        b_specs=[pl.BlockSpec((k, tn), lambda j, i: (w_row, w1_col * nj + j)),
                 pl.BlockSpec((k, tn), lambda j, i: (w_row, w2_col * nj + j))],
        out_shapes=[jax.ShapeDtypeStruct((m, n_out), out_dtype)],
        out_specs=[pl.BlockSpec((tm, tn), lambda j, i: (i, j))],
        epilogue=epilogue, extras=extras, extra_specs=especs, name=name)
    return out


def _rowwise_kernel(*refs, nin, fn):
    ins = [r[...] for r in refs[:nin]]
    outs = fn(*ins)
    for o_ref, val in zip(refs[nin:], outs):
        o_ref[...] = val.astype(o_ref.dtype)


def _rowwise(fn, geo, *, rows=(), params=(), conds=(), outs, tm, n_rows=None, row_block0=0, name="rowwise"):
    n_rows = n_rows or rows[0][0].shape[0]
    cond = _cond_of_block(geo, tm)
    in_specs, args, sizes = [], [], []
    for arr, w, cb in rows:
        in_specs.append(pl.BlockSpec((tm, w), lambda i, cb=cb: (i + row_block0, cb)))
        args.append(arr)
        sizes.append(_nbytes((tm, w), arr.dtype))
    for arr, w, cb in params:
        in_specs.append(pl.BlockSpec((1, w), lambda i, cb=cb: (0, cb)))
        args.append(arr)
    for arr in conds:
        w = arr.shape[-1]
        in_specs.append(pl.BlockSpec((None, 1, w), lambda i: (cond(i + row_block0), 0, 0)))
        args.append(arr)
    out_shapes = [jax.ShapeDtypeStruct((n_rows, w), dt) for w, dt in outs]
    out_specs = [pl.BlockSpec((tm, w), lambda i: (i, 0)) for w, dt in outs]
    sizes += [_nbytes((tm, w), dt) for w, dt in outs]
    return pl.pallas_call(
        functools.partial(_rowwise_kernel, nin=len(args), fn=fn),
        grid=(n_rows // tm,), in_specs=in_specs, out_specs=out_specs, out_shape=out_shapes, name=name,
        compiler_params=pltpu.CompilerParams(
            dimension_semantics=("arbitrary",), vmem_limit_bytes=_vmem_limit(*sizes, *sizes)),
    )(*args)


def _rms(x, g):
    return x * lax.rsqrt(jnp.mean(x * x, axis=-1, keepdims=True) + EPS) * g


def _norm_mod_fn(x, g, sc, sh):
    return (_rms(x, g) * (1.0 + sc) + sh,)


def _norm_mod(x, g, sc, sh, geo):
    (u,) = _rowwise(_norm_mod_fn, geo, rows=[(x, D_MODEL, 0)], params=[(g, D_MODEL, 0)],
                    conds=[sc, sh], outs=[(D_MODEL, BF16)], tm=512, name="norm_mod")
    return u


def _norm_mod_route_fn(x, g, router, sc, sh):
    u = _rms(x, g) * (1.0 + sc) + sh
    logits = jnp.dot(u, router, preferred_element_type=F32, precision=HIGHEST)
    lane = lax.broadcasted_iota(jnp.int32, logits.shape, 1)
    neg = jnp.float32(-jnp.inf)
    logits = jnp.where(lane < N_EXPERTS, logits, neg)
    v1 = jnp.max(logits, axis=-1, keepdims=True)
    i1 = jnp.min(jnp.where(logits == v1, lane, LANES), axis=-1, keepdims=True)
    rest = jnp.where(lane == i1, neg, logits)
    v2 = jnp.max(rest, axis=-1, keepdims=True)
    i2 = jnp.min(jnp.where(rest == v2, lane, LANES), axis=-1, keepdims=True)
    e2 = jnp.exp(v2 - v1)
    p1 = 1.0 / (1.0 + e2)
    p2 = e2 / (1.0 + e2)
    info = jnp.where(lane == 0, i1.astype(F32),
                     jnp.where(lane == 1, i2.astype(F32),
                               jnp.where(lane == 2, p1, jnp.where(lane == 3, p2, 0.0))))
    return u, info


def _norm_mod_route_kernel(x_ref, g_ref, r_ref, sc_ref, sh_ref, u_ref, info_ref):
    u, info = _norm_mod_route_fn(x_ref[...], g_ref[...], r_ref[...], sc_ref[...], sh_ref[...])
    for s in range(u_ref.shape[1]):
        u_ref[:, s, :] = u[:, s * LANES:(s + 1) * LANES]
    info_ref[...] = info


def _norm_mod_route(x, g, sc, sh, router, geo):
    tm = 256
    cond = _cond_of_block(geo, tm)
    n = x.shape[0]
    router_p = jnp.zeros((D_MODEL, LANES), F32).at[:, :N_EXPERTS].set(router)
    row = pl.BlockSpec((tm, D_MODEL), lambda i: (i, 0))
    cspec = pl.BlockSpec((None, 1, D_MODEL), lambda i: (cond(i), 0, 0))
    groups = D_MODEL // LANES
    return pl.pallas_call(
        _norm_mod_route_kernel,
        grid=(n // tm,),
        in_specs=[row, pl.BlockSpec((1, D_MODEL), lambda i: (0, 0)),
                  pl.BlockSpec((D_MODEL, LANES), lambda i: (0, 0)), cspec, cspec],
        out_specs=[pl.BlockSpec((tm, groups, LANES), lambda i: (i, 0, 0)),
                   pl.BlockSpec((tm, LANES), lambda i: (i, 0))],
        out_shape=[jax.ShapeDtypeStruct((n, groups, LANES), F32), jax.ShapeDtypeStruct((n, LANES), F32)],
        name="norm_mod_route", compiler_params=pltpu.CompilerParams(dimension_semantics=("arbitrary",)),
    )(x, g, router_p, sc, sh)


def _dwconv_kernel(x_ref, w_ref, *rest, taps, geo, rb, post, has_bias):
    if has_bias:
        b_ref, o_ref, pad_ref = rest
    else:
        o_ref, pad_ref = rest
    half = (taps - 1) // 2
    tc = x_ref.shape[1]
    n_prompt_blocks = _tp(geo) // rb
    first = CONV_HALO - half
    win0 = first // SUBLANES * SUBLANES
    shift0 = first - win0
    win_rows = -(-(shift0 + taps - 1 + CONV_ROWS) // SUBLANES) * SUBLANES

    def run(nseq, ln):
        for s in range(nseq):
            base = s * ln
            zeros = jnp.zeros((CONV_HALO, tc), F32)
            pad_ref[0:CONV_HALO, :] = zeros
            pad_ref[CONV_HALO + ln:2 * CONV_HALO + ln, :] = zeros
            pad_ref[CONV_HALO:CONV_HALO + ln, :] = x_ref[base:base + ln, :]
            for r0 in range(0, ln, CONV_ROWS):
                acc = jnp.zeros((CONV_ROWS, tc), F32)
                win = pad_ref[win0 + r0:win0 + r0 + win_rows, :]
                for r in range(SUBLANES):
                    group = [kk for kk in range(taps) if (shift0 + kk) % SUBLANES == r]
                    if not group:
                        continue
                    rot = win if r == 0 else pltpu.roll(win, win_rows - r, 0)
                    for kk in group:
                        m0 = (shift0 + kk) // SUBLANES * SUBLANES
                        acc = acc + w_ref[kk:kk + 1, :] * rot[m0:m0 + CONV_ROWS]
                if has_bias:
                    acc = acc + b_ref[...]
                o_ref[base + r0:base + r0 + CONV_ROWS, :] = post(acc)

    i = pl.program_id(0)
    if n_prompt_blocks > 0:
        @pl.when(i < n_prompt_blocks)
        def _():
            run(rb // geo.l_p, geo.l_p)

    @pl.when(i >= n_prompt_blocks)
    def _():
        run(1, geo.l_s)


def _dwconv(x, w, geo, *, n_ch, bias=None, post=None, tc=256):
    taps = w.shape[0]
    rb = geo.l_s
    t = x.shape[0]
    post = post or (lambda v: v)
    in_specs = [pl.BlockSpec((rb, tc), lambda i, j: (i, j)), pl.BlockSpec((taps, tc), lambda i, j: (0, j))]
    args = [x, w]
    if bias is not None:
        in_specs.append(pl.BlockSpec((1, tc), lambda i, j: (0, j)))
        args.append(bias)
    blk = _nbytes((rb, tc), F32)
    return pl.pallas_call(
        functools.partial(_dwconv_kernel, taps=taps, geo=geo, rb=rb, post=post, has_bias=bias is not None),
        grid=(t // rb, n_ch // tc), in_specs=in_specs,
        out_specs=pl.BlockSpec((rb, tc), lambda i, j: (i, j)),
        out_shape=jax.ShapeDtypeStruct((t, n_ch), F32),
        scratch_shapes=[pltpu.VMEM((rb + 2 * CONV_HALO, tc), F32)], name="dwconv%d" % taps,
        compiler_params=pltpu.CompilerParams(
            dimension_semantics=("arbitrary", "arbitrary"), vmem_limit_bytes=_vmem_limit(blk, blk, blk)),
    )(*args)


def _dft_tables(ln):
    kb = 64
    t3 = jnp.arange(ln, dtype=jnp.int32)[None, None, :]
    a3 = jnp.arange(ln // kb, dtype=jnp.int32)[:, None, None]
    b3 = jnp.arange(kb, dtype=jnp.int32)[None, :, None]
    ang_a = ((kb * a3 * t3) % (2 * ln)).astype(F32) * (math.pi / ln)
    ang_b = ((b3 * t3) % (2 * ln)).astype(F32) * (math.pi / ln)
    ca, sa, cb, sb = jnp.cos(ang_a), jnp.sin(ang_a), jnp.cos(ang_b), jnp.sin(ang_b)
    cos = (ca * cb - sa * sb).reshape(ln, ln)
    sin = (sa * cb + ca * sb).reshape(ln, ln)
    k = jnp.arange(ln, dtype=jnp.int32)[:, None]
    t = jnp.arange(ln, dtype=jnp.int32)[None, :]
    alt_t = jnp.where(t % 2 == 0, 1.0, -1.0).astype(F32)
    fwd_s = jnp.where(k == 0, alt_t, -sin)
    fwd = jnp.concatenate([cos, fwd_s], axis=0).astype(BF16)
    scale = 1.0 / (2 * ln)
    inv_c = jnp.where(k == 0, 1.0, 2.0 * cos).T * scale
    inv_s = jnp.where(k == 0, alt_t, -2.0 * sin).T * scale
    return fwd, inv_c.astype(BF16), inv_s.astype(BF16)


def _filter_hidden_kernel(w1_ref, b1_ref, w2_ref, b2_ref, o_ref, *, ln):
    pos = lax.broadcasted_iota(jnp.int32, (ln, LANES), 0).astype(F32) / ln
    lane = lax.broadcasted_iota(jnp.int32, (ln, LANES), 1)
    band = jnp.where(lane <= HY_BANDS, lane, lane - HY_BANDS).astype(F32)
    ang = (2.0 * math.pi) * pos * band
    feats = jnp.where(lane == 0, pos,
                      jnp.where(lane <= HY_BANDS, jnp.cos(ang),
                                jnp.where(lane <= 2 * HY_BANDS, jnp.sin(ang), 0.0)))
    h = jnp.sin(jnp.dot(feats, w1_ref[...], preferred_element_type=F32, precision=HIGHEST) + b1_ref[...])
    o_ref[...] = jnp.sin(jnp.dot(h, w2_ref[...], preferred_element_type=F32, precision=HIGHEST) + b2_ref[...])


def _filter_kernel(h_ref, w3_ref, hw_ref, ss_ref, *, ln):
    tn = hw_ref.shape[1]
    j = pl.program_id(0)
    h = jnp.dot(h_ref[...], w3_ref[...], preferred_element_type=F32, precision=HIGHEST)
    ch = (j * tn) % D_MODEL + lax.broadcasted_iota(jnp.int32, (1, tn), 1)
    step = (HY_FAST_DECAY - HY_SLOW_DECAY) / (D_MODEL - 1)
    deltas = HY_SLOW_DECAY + ch.astype(F32) * step
    tcol = lax.broadcasted_iota(jnp.int32, (ln, 1), 0).astype(F32) / ln
    h = h * (jnp.exp(-tcol * deltas) + HY_SHIFT)
    hw_ref[...] = h
    ss_ref[...] = jnp.sum(h * h, axis=0, keepdims=True)


def _hyena_filters(ln, w1, b1, w2, b2, w3, fwd):
    hid = w1.shape[1]
    ncol = w3.shape[1]
    tn = 512
    w1p = jnp.zeros((LANES, hid), F32).at[:w1.shape[0]].set(w1)
    full = lambda shape: pl.BlockSpec(shape, lambda j: (0, 0))
    hidden = pl.pallas_call(
        functools.partial(_filter_hidden_kernel, ln=ln),
        grid=(1,),
        in_specs=[full((LANES, hid)), full((1, hid)), full((hid, hid)), full((1, hid))],
        out_specs=full((ln, hid)),
        out_shape=jax.ShapeDtypeStruct((ln, hid), F32),
        name="hy_filter_hidden",
        compiler_params=pltpu.CompilerParams(dimension_semantics=("arbitrary",)),
    )(w1p, b1[None, :], w2, b2[None, :])
    hw, ss = pl.pallas_call(
        functools.partial(_filter_kernel, ln=ln),
        grid=(ncol // tn,),
        in_specs=[full((ln, hid)), pl.BlockSpec((hid, tn), lambda j: (0, j))],
        out_specs=[pl.BlockSpec((ln, tn), lambda j: (0, j)), pl.BlockSpec((1, tn), lambda j: (0, j))],
        out_shape=[jax.ShapeDtypeStruct((ln, ncol), F32), jax.ShapeDtypeStruct((1, ncol), F32)],
        name="hy_filter_cols",
        compiler_params=pltpu.CompilerParams(dimension_semantics=("arbitrary",),
                                             vmem_limit_bytes=_vmem_limit(_nbytes((ln, tn), F32) * 4)),
    )(hidden, w3)
    spec = _linear(fwd, hw, out_dtype=F32, tm=_pick(2 * ln, (1024, 512)), name="hy_filter_dft")
    half = ncol // 2
    tk = _pick(ln, (512, 256))

    def assemble(fc, fs, bc, bs, ssf, ssb):
        scale = lax.rsqrt(ssf + ssb + EPS)
        row0 = (pl.program_id(0) == 0) & (lax.broadcasted_iota(jnp.int32, fc.shape, 0) == 0)
        gc = (fc + bc) * scale
        gs = jnp.where(row0, fs + bs, fs - bs) * scale
        return gc, gs

    nk = ln // tk
    nj = half // tn
    return pl.pallas_call(
        functools.partial(_rowwise_kernel, nin=6, fn=assemble),
        grid=(nk, nj),
        in_specs=[pl.BlockSpec((tk, tn), lambda i, j: (i, j)),
                  pl.BlockSpec((tk, tn), lambda i, j: (nk + i, j)),
                  pl.BlockSpec((tk, tn), lambda i, j: (i, nj + j)),
                  pl.BlockSpec((tk, tn), lambda i, j: (nk + i, nj + j)),
                  pl.BlockSpec((1, tn), lambda i, j: (0, j)),
                  pl.BlockSpec((1, tn), lambda i, j: (0, nj + j))],
        out_specs=[pl.BlockSpec((tk, tn), lambda i, j: (i, j)), pl.BlockSpec((tk, tn), lambda i, j: (i, j))],
        out_shape=[jax.ShapeDtypeStruct((ln, half), F32)] * 2, name="hy_filter_assemble",
        compiler_params=pltpu.CompilerParams(dimension_semantics=("arbitrary", "arbitrary")),
    )(spec, spec, spec, spec, ss, ss)


def _dft_fwd_kernel(fc_ref, fs_ref, z_ref, gc_ref, gs_ref, yc_ref, ys_ref, zb_ref, *, nb):
    @pl.when(pl.program_id(2) == 0)
    def _():
        zb_ref[...] = z_ref[...].astype(BF16)

    fc = fc_ref[...]
    fs = fs_ref[...]
    gcb = gc_ref[...]
    gsb = gs_ref[...]
    row_0 = (pl.program_id(2) == 0) & (lax.broadcasted_iota(jnp.int32, gcb.shape, 0) == 0)
    for s in range(nb):
        zc = jnp.dot(fc, zb_ref[s], preferred_element_type=F32)
        zs = jnp.dot(fs, zb_ref[s], preferred_element_type=F32)
        yc_ref[s] = jnp.where(row_0, zc * gcb, zc * gcb - zs * gsb).astype(yc_ref.dtype)
        ys_ref[s] = jnp.where(row_0, zs * gsb, zc * gsb + zs * gcb).astype(ys_ref.dtype)


def _dft_inv_kernel(ic_ref, is_ref, yc_ref, ys_ref, zin_ref, gate_ref, bias_ref, o_ref, *, nb):
    ic = ic_ref[...]
    isn = is_ref[...]
    bias = bias_ref[...]
    for s in range(nb):
        acc = jnp.dot(ic, yc_ref[s], preferred_element_type=F32)
        acc = acc + jnp.dot(isn, ys_ref[s], preferred_element_type=F32)
        o_ref[s] = (gate_ref[s] * (acc + zin_ref[s] * bias)).astype(o_ref.dtype)


def _long_conv_gated(z, z_col, pc, gate_col, gc, gs, order, bias, tables, *, ln, nseq, row0, out_dtype):
    fwd, inv_c, inv_s = tables
    d = D_MODEL
    tn = 512
    nj = d // tn
    tk = _pick(ln, (512, 256))
    nk = ln // tk
    nb = max(n for n in (8, 4, 2, 1) if nseq % n == 0 and n * ln <= 2048)
    z3 = z.reshape(z.shape[0] // ln, ln, z.shape[1])
    pc3 = pc.reshape(pc.shape[0] // ln, ln, pc.shape[1])
    z_sb0 = (row0 // ln if z.shape[0] != nseq * ln else 0) // nb
    pc_sb0 = (row0 // ln) // nb
    seq_blk = (nb, ln, tn)
    out_blk = (nb, tk, tn)
    cparams = pltpu.CompilerParams(
        dimension_semantics=("arbitrary",) * 3,
        vmem_limit_bytes=_vmem_limit(_nbytes(seq_blk, F32) * 2, _nbytes(out_blk, F32) * 3, _nbytes((tk, ln), BF16) * 2))
    yc, ys = pl.pallas_call(
        functools.partial(_dft_fwd_kernel, nb=nb), grid=(nseq // nb, nj, nk),
        in_specs=[pl.BlockSpec((tk, ln), lambda b, j, i: (i, 0)),
                  pl.BlockSpec((tk, ln), lambda b, j, i: (nk + i, 0)),
                  pl.BlockSpec(seq_blk, lambda b, j, i: (z_sb0 + b, 0, z_col * nj + j)),
                  pl.BlockSpec((tk, tn), lambda b, j, i: (i, order * nj + j)),
                  pl.BlockSpec((tk, tn), lambda b, j, i: (i, order * nj + j))],
        out_specs=[pl.BlockSpec(out_blk, lambda b, j, i: (b, i, j))] * 2,
        out_shape=[jax.ShapeDtypeStruct((nseq, ln, d), BF16)] * 2,
        scratch_shapes=[pltpu.VMEM(seq_blk, BF16)], name="hy_dft_fwd", compiler_params=cparams,
    )(fwd, fwd, z3, gc, gs)
    out = pl.pallas_call(
        functools.partial(_dft_inv_kernel, nb=nb), grid=(nseq // nb, nj, nk),
        in_specs=[pl.BlockSpec((tk, ln), lambda b, j, i: (i, 0)),
                  pl.BlockSpec((tk, ln), lambda b, j, i: (i, 0)),
                  pl.BlockSpec(seq_blk, lambda b, j, i: (b, 0, j)),
                  pl.BlockSpec(seq_blk, lambda b, j, i: (b, 0, j)),
                  pl.BlockSpec(out_blk, lambda b, j, i: (z_sb0 + b, i, z_col * nj + j)),
                  pl.BlockSpec(out_blk, lambda b, j, i: (pc_sb0 + b, i, gate_col * nj + j)),
                  pl.BlockSpec((None, 1, tn), lambda b, j, i: (order, 0, j))],
        out_specs=pl.BlockSpec(out_blk, lambda b, j, i: (b, i, j)),
        out_shape=jax.ShapeDtypeStruct((nseq, ln, d), out_dtype),
        name="hy_dft_inv", compiler_params=cparams,
    )(inv_c, inv_s, yc, ys, z3, pc3, bias[:, None, :])
    return out.reshape(nseq * ln, d)


def _hyena_mixer(u, geo, w_in, short, w1, b1, w2, b2, w3, filt_bias, w_out, res_extras):
    d = D_MODEL
    p = _linear(u, w_in, out_dtype=F32, name="hy_in")
    pc = _dwconv(p, short, geo, n_ch=3 * d)
    parts = []
    for ln, nseq, row0 in ((geo.l_p, geo.n_p, 0), (geo.l_s, geo.n_s, _tp(geo))):
        if nseq == 0:
            continue
        tables = _dft_tables(ln)
        gc, gs = _hyena_filters(ln, w1, b1, w2, b2, w3, tables[0])
        z1 = _long_conv_gated(pc, 0, pc, 1, gc, gs, 0, filt_bias, tables, ln=ln, nseq=nseq, row0=row0,
                              out_dtype=F32)
        z2 = _long_conv_gated(z1, 0, pc, 2, gc, gs, 1, filt_bias, tables, ln=ln, nseq=nseq, row0=row0,
                              out_dtype=BF16)
        parts.append(z2)
    z = jnp.concatenate(parts, axis=0) if len(parts) > 1 else parts[0]
    return _linear(z, w_out, out_dtype=F32, epilogue=_ep_residual, extras=res_extras,
                   extra_specs_fn=_residual_specs(geo), geo=geo, name="hy_out")


def _head_linear(a, ws, scales):
    t = a.shape[0]
    tm = _pick(t, (1024, 512, 256))
    nw = len(ws)
    return _mm(
        [a], [w.reshape(ML_HEADS * ML_DH, ML_DH) for w in ws], [(0, n) for n in range(nw)],
        grid=(ML_HEADS, t // tm),
        a_specs=[pl.BlockSpec((tm, ML_DH), lambda h, i: (i, h))],
        b_specs=[pl.BlockSpec((ML_DH, ML_DH), lambda h, i: (h, 0))] * nw,
        out_shapes=[jax.ShapeDtypeStruct((t, ML_HEADS * ML_DH), BF16)] * nw,
        out_specs=[pl.BlockSpec((tm, ML_DH), lambda h, i: (i, h))] * nw,
        epilogue=lambda accs, ex: tuple(acc if s is None else acc * s for acc, s in zip(accs, scales)),
        name="ml_head_linear")


def _gate_prep_kernel(pre_ref, o_ref):
    pre = pre_ref[...]
    lane = lax.broadcasted_iota(jnp.int32, pre.shape, 1)
    is_f = (lane % 16) >= ML_HEADS
    lf = jnp.minimum(pre, 0.0) - jnp.log(1.0 + jnp.exp(-jnp.abs(pre)))
    r = lax.broadcasted_iota(jnp.int32, (CHUNK, CHUNK), 0)
    c = lax.broadcasted_iota(jnp.int32, (CHUNK, CHUNK), 1)
    lower = (c <= r).astype(F32)
    upper = (c >= r).astype(F32)
    pre_sum = jnp.dot(lower, lf, preferred_element_type=F32, precision=HIGHEST)
    suf_sum = jnp.dot(upper, lf, preferred_element_type=F32, precision=HIGHEST)
    cum = jnp.where(lane < 16, pre_sum, suf_sum)
    o_ref[...] = jnp.where(is_f, cum, pre)


def _mlstm_scan_kernel(*refs, has_init, nc):
    if has_init:
        q_ref, k_ref, v_ref, gc_ref, gr_ref, c0_ref, n0_ref, m0_ref, h_ref, c_ref, n_ref, m_ref = refs
    else:
        q_ref, k_ref, v_ref, gc_ref, gr_ref, h_ref, c_ref, n_ref, m_ref = refs
    dr = pl.program_id(0)
    ci = pl.program_id(2)

    carry = has_init or nc > 1
    if carry:
        @pl.when(ci == 0)
        def _():
            if has_init:
                c_ref[...] = c0_ref[...]
                n_ref[...] = n0_ref[...]
                m_ref[...] = m0_ref[...]
            else:
                c_ref[...] = jnp.zeros_like(c_ref)
                n_ref[...] = jnp.zeros_like(n_ref)
                m_ref[...] = jnp.zeros_like(m_ref)

    gcv = gc_ref[...]
    lane = lax.broadcasted_iota(jnp.int32, gcv.shape, 1)
    rowi = lax.broadcasted_iota(jnp.int32, (CHUNK, CHUNK), 0)
    coli = lax.broadcasted_iota(jnp.int32, (CHUNK, CHUNK), 1)
    mask = (rowi - coli) * (1 - 2 * dr) >= 0
    for hh in range(ML_HEADS):
        cols = slice(hh * ML_DH, (hh + 1) * ML_DH)
        q = q_ref[:, cols]
        k = k_ref[:, cols]
        v = v_ref[:, cols]
        li_idx = dr * 16 + hh
        b_idx = li_idx + ML_HEADS
        bcol = jnp.sum(jnp.where(lane == b_idx, gcv, 0.0), axis=1, keepdims=True)
        licol = jnp.sum(jnp.where(lane == li_idx, gcv, 0.0), axis=1, keepdims=True)
        brow = gr_ref[pl.ds(b_idx, 1), :]
        lirow = gr_ref[pl.ds(li_idx, 1), :]
        m_prev = m_ref[hh] if carry else jnp.zeros((1, 1), F32)

        dmat = jnp.where(mask, bcol - brow + lirow, -jnp.inf)
        a = bcol + m_prev
        mt = jnp.maximum(a, jnp.max(dmat, axis=1, keepdims=True))
        qk = lax.dot_general(q, k, (((1,), (1,)), ((), ())), preferred_element_type=F32)
        s = qk * jnp.exp(dmat - mt)
        num = jnp.dot(s.astype(BF16), v, preferred_element_type=F32)
        den = jnp.sum(s, axis=1, keepdims=True)
        if carry:
            w_inter = jnp.exp(a - mt)
            c_prev = c_ref[hh]
            n_prev = n_ref[hh]
            num = num + w_inter * jnp.dot(q, c_prev.astype(BF16), preferred_element_type=F32)
            den = den + w_inter * jnp.sum(q.astype(F32) * n_prev, axis=1, keepdims=True)
        h_ref[:, cols] = num / jnp.maximum(jnp.abs(den), jnp.exp(-mt))

        b_last = jnp.min(bcol, axis=0, keepdims=True)
        dl = b_last - bcol + licol
        m_new = jnp.maximum(b_last + m_prev, jnp.max(dl, axis=0, keepdims=True))
        kw = k.astype(F32) * jnp.exp(dl - m_new)
        c_new = lax.dot_general(kw.astype(BF16), v, (((0,), (0,)), ((), ())), preferred_element_type=F32)
        n_new = jnp.sum(kw, axis=0, keepdims=True)
        if carry:
            w_old = jnp.exp(b_last + m_prev - m_new)
            c_new = w_old * c_prev + c_new
            n_new = w_old * n_prev + n_new
        c_ref[hh] = c_new
        n_ref[hh] = n_new
        m_ref[hh] = m_new


def _mlstm_scan(q, k, v, gcols, grows, *, ln, nseq, row0, init=None):
    nc = ln // CHUNK
    rb0 = row0 // CHUNK
    hd = ML_HEADS * ML_DH

    def blk(dr, b, c):
        return rb0 + b * nc + c + dr * (nc - 1 - 2 * c)

    row_spec = pl.BlockSpec((CHUNK, hd), lambda dr, b, c: (blk(dr, b, c), 0))
    in_specs = [row_spec, row_spec, row_spec,
                pl.BlockSpec((CHUNK, LANES), lambda dr, b, c: (blk(dr, b, c), 0)),
                pl.BlockSpec((LANES, CHUNK), lambda dr, b, c: (0, blk(dr, b, c)))]
    args = [q, k, v, gcols, grows]
    st_c = pl.BlockSpec((None, None, ML_HEADS, ML_DH, ML_DH), lambda dr, b, c: (b, dr, 0, 0, 0))
    st_n = pl.BlockSpec((None, None, ML_HEADS, 1, ML_DH), lambda dr, b, c: (b, dr, 0, 0, 0))
    st_m = pl.BlockSpec((None, None, ML_HEADS, 1, 1), lambda dr, b, c: (b, dr, 0, 0, 0))
    if init is not None:
        in_specs += [st_c, st_n, st_m]
        args += list(init)
    state_bytes = _nbytes((ML_HEADS, ML_DH, ML_DH), F32)
    return pl.pallas_call(
        functools.partial(_mlstm_scan_kernel, has_init=init is not None, nc=nc),
        grid=(2, nseq, nc), in_specs=in_specs,
        out_specs=[pl.BlockSpec((None, CHUNK, hd),
                                lambda dr, b, c: (dr, b * nc + c + dr * (nc - 1 - 2 * c), 0)),
                   st_c, st_n, st_m],
        out_shape=[jax.ShapeDtypeStruct((2, nseq * ln, hd), F32),
                   jax.ShapeDtypeStruct((nseq, 2, ML_HEADS, ML_DH, ML_DH), F32),
                   jax.ShapeDtypeStruct((nseq, 2, ML_HEADS, 1, ML_DH), F32),
                   jax.ShapeDtypeStruct((nseq, 2, ML_HEADS, 1, 1), F32)],
        name="mlstm_scan_l%d" % ln,
        compiler_params=pltpu.CompilerParams(
            dimension_semantics=("arbitrary",) * 3,
            vmem_limit_bytes=_vmem_limit(3 * _nbytes((CHUNK, hd), BF16), _nbytes((CHUNK, hd), F32),
                                         2 * state_bytes)),
    )(*args)


def _mlstm_post_fn(hf, hb, xc, z, norm_g, skip):
    h = hf + hb
    outs = []
    for hd in range(ML_HEADS):
        sl = slice(hd * ML_DH, (hd + 1) * ML_DH)
        hh = h[:, sl]
        mu = jnp.mean(hh, axis=-1, keepdims=True)
        var = jnp.mean(jnp.square(hh - mu), axis=-1, keepdims=True)
        outs.append((hh - mu) * lax.rsqrt(var + EPS))
    hn = jnp.concatenate(outs, axis=-1)
    y = hn * norm_g + skip * xc
    return (y * _silu(z),)


def _mlstm_mixer(u, geo, state, w_up, conv_w, w_q, w_k, w_v, w_gate, b_gate, norm_g, skip, w_down,
                 res_extras):
    inner = ML_HEADS * ML_DH
    t = u.shape[0]
    up = _linear(u, w_up, out_dtype=F32, name="ml_up")
    xc = _dwconv(up, conv_w, geo, n_ch=inner, post=_silu)
    q, k = _head_linear(xc, [w_q, w_k], [ML_DH ** -0.5, None])
    (v,) = _head_linear(up, [w_v], [None])
    wg = jnp.transpose(w_gate, (1, 0, 2)).reshape(3 * inner, 4 * ML_HEADS)
    wg = jnp.zeros((3 * inner, LANES), F32).at[:, :4 * ML_HEADS].set(wg)
    bg = jnp.zeros((1, LANES), F32).at[0, :4 * ML_HEADS].set(b_gate.reshape(-1))
    tm = _pick(t, (1024, 512, 256))
    (pre,) = _mm(
        [q, k, v], [wg, wg, wg], [(0, 0), (1, 1), (2, 2)], grid=(1, t // tm),
        a_specs=[pl.BlockSpec((tm, inner), lambda j, i: (i, 0))] * 3,
        b_specs=[pl.BlockSpec((inner, LANES), lambda j, i, r=r: (r, 0)) for r in range(3)],
        out_shapes=[jax.ShapeDtypeStruct((t, LANES), F32)],
        out_specs=[pl.BlockSpec((tm, LANES), lambda j, i: (i, 0))],
        epilogue=lambda accs, ex: (accs[0] + accs[1] + accs[2] + ex[0],),
        extras=[bg], extra_specs=[pl.BlockSpec((1, LANES), lambda j, i: (0, 0))], name="ml_gates")
    gcols = pl.pallas_call(
        _gate_prep_kernel, grid=(t // CHUNK,),
        in_specs=[pl.BlockSpec((CHUNK, LANES), lambda i: (i, 0))],
        out_specs=pl.BlockSpec((CHUNK, LANES), lambda i: (i, 0)),
        out_shape=jax.ShapeDtypeStruct((t, LANES), F32), name="mlstm_gate_prep",
        compiler_params=pltpu.CompilerParams(dimension_semantics=("arbitrary",)),
    )(pre)
    grows = gcols.T
    c0, n0, m0 = state
    hp, c_p, n_p, m_p = _mlstm_scan(q, k, v, gcols, grows, ln=geo.l_p, nseq=geo.n_p, row0=0)
    hs, _, _, _ = _mlstm_scan(q, k, v, gcols, grows, ln=geo.l_s, nseq=geo.n_s, row0=_tp(geo),
                              init=(c0, n0[:, :, :, None, :], m0[:, :, :, None, None]))
    ys = []
    for hdir, nrows, row0 in ((hp, _tp(geo), 0), (hs, t - _tp(geo), _tp(geo))):
        tmr = 256
        nb = nrows // tmr
        rb0 = row0 // tmr
        y = pl.pallas_call(
            functools.partial(_rowwise_kernel, nin=6, fn=_mlstm_post_fn),
            grid=(nb,),
            in_specs=[pl.BlockSpec((None, tmr, inner), lambda i: (0, i, 0)),
                      pl.BlockSpec((None, tmr, inner), lambda i: (1, i, 0)),
                      pl.BlockSpec((tmr, inner), lambda i: (rb0 + i, 0)),
                      pl.BlockSpec((tmr, inner), lambda i: (rb0 + i, 1)),
                      pl.BlockSpec((1, inner), lambda i: (0, 0)),
                      pl.BlockSpec((1, inner), lambda i: (0, 0))],
            out_specs=[pl.BlockSpec((tmr, inner), lambda i: (i, 0))],
            out_shape=[jax.ShapeDtypeStruct((nrows, inner), BF16)], name="mlstm_post",
            compiler_params=pltpu.CompilerParams(
                dimension_semantics=("arbitrary",),
                vmem_limit_bytes=_vmem_limit(5 * _nbytes((tmr, inner), F32))),
        )(hdir, hdir, xc, up, norm_g[None, :], skip[None, :])[0]
        ys.append(y)
    y = jnp.concatenate(ys, axis=0)
    x_new = _linear(y, w_down, out_dtype=F32, epilogue=_ep_residual, extras=res_extras,
                    extra_specs_fn=_residual_specs(geo), geo=geo, name="ml_down")
    return x_new, (c_p, n_p, m_p)


def _ln_silu_fn(y, g, b):
    mu = jnp.mean(y, axis=-1, keepdims=True)
    var = jnp.mean(jnp.square(y - mu), axis=-1, keepdims=True)
    return (_silu((y - mu) * lax.rsqrt(var + EPS) * g + b),)


def _conformer_mixer(u, geo, w1, b1, dw, b_dw, ln_g, ln_b, w2, b2, res_extras):
    d = D_MODEL
    b1r = b1[None, :]

    def glu(accs, ex):
        return ((accs[0] + ex[0]) * _sigmoid(accs[1] + ex[1]),)

    def especs(i_of, j_of, tm, tn):
        nj = d // tn
        return [pl.BlockSpec((1, tn), lambda *g: (0, j_of(*g))),
                pl.BlockSpec((1, tn), lambda *g: (0, nj + j_of(*g)))]

    y = _dual_linear(u, w1, w1, epilogue=glu, out_dtype=F32, n_out=d, w1_col=0, w2_col=1,
                     extras=[b1r, b1r], extra_specs_fn=especs, name="cv_glu")
    y = _dwconv(y, dw, geo, n_ch=d, bias=b_dw[None, :])
    (y,) = _rowwise(_ln_silu_fn, geo, rows=[(y, d, 0)], params=[(ln_g[None, :], d, 0), (ln_b[None, :], d, 0)],
                    outs=[(d, BF16)], tm=512, name="cv_ln_silu")
    return _linear(y, w2, out_dtype=F32, epilogue=_ep_residual_bias, extras=list(res_extras) + [b2[None, :]],
                   extra_specs_fn=_residual_bias_specs(geo), geo=geo, name="cv_out")


def _ep_swiglu(accs, ex):
    return (_silu(accs[0]) * accs[1],)


def _dense_ffn(u, geo, wg, wu, wd, layer, res_extras):
    _, d, f = wg.shape
    h = _dual_linear(u, wg.reshape(-1, f), wu.reshape(-1, f), epilogue=_ep_swiglu, out_dtype=BF16, n_out=f,
                     w_row=layer, name="ffn_up")
    return _linear(h, wd.reshape(-1, d), out_dtype=F32, epilogue=_ep_residual, extras=res_extras,
                   extra_specs_fn=_residual_specs(geo), geo=geo, k=f, w_row=layer, name="ffn_down")


def _lane_col(blk, idx):
    lane = lax.broadcasted_iota(jnp.int32, blk.shape, 1)
    return jnp.sum(jnp.where(lane == idx, blk, 0.0), axis=1, keepdims=True)


def _route_rank_kernel(info_ref, rank_ref, cnt_ref, carry_ref):
    @pl.when(pl.program_id(0) == 0)
    def _():
        carry_ref[...] = jnp.zeros_like(carry_ref)

    info = info_ref[...]
    lane = lax.broadcasted_iota(jnp.int32, info.shape, 1)
    lanef = lane.astype(F32)
    i1 = _lane_col(info, 0)
    i2 = _lane_col(info, 1)
    sel = jnp.where((lanef == i1) | (lanef == i2), 1.0, 0.0)
    r = lax.broadcasted_iota(jnp.int32, (CHUNK, CHUNK), 0)
    c = lax.broadcasted_iota(jnp.int32, (CHUNK, CHUNK), 1)
    strict = (c < r).astype(BF16)
    before = jnp.dot(strict, sel.astype(BF16), preferred_element_type=F32) + carry_ref[...]
    r1 = jnp.sum(jnp.where(lanef == i1, before, 0.0), axis=1, keepdims=True)
    r2 = jnp.sum(jnp.where(lanef == i2, before, 0.0), axis=1, keepdims=True)
    rank_ref[...] = jnp.where(lane == 0, r1, jnp.where(lane == 1, r2, 0.0))
    carry = carry_ref[...] + jnp.sum(sel, axis=0, keepdims=True)
    carry_ref[...] = carry
    cnt_ref[...] = carry


def _moe_gather_kernel(te_ref, inv_ref, u_hbm, xs_ref, buf, sem, *, nt):
    tm, groups, _ = buf.shape
    i = pl.program_id(0)

    def token_copy(src_tok, dst_row):
        return pltpu.make_async_copy(u_hbm.at[src_tok], buf.at[dst_row], sem)

    def issue(r2, carry):
        token_copy(inv_ref[0, 2 * r2], 2 * r2).start(priority=0)
        token_copy(inv_ref[0, 2 * r2 + 1], 2 * r2 + 1).start(priority=1)
        return carry

    def drain(r, carry):
        token_copy(0, r).wait()
        return carry

    @pl.when(i < te_ref[nt])
    def _():
        lax.fori_loop(0, tm // 2, issue, 0, unroll=4)
        lax.fori_loop(0, tm, drain, 0, unroll=8)
        xs_ref[...] = jnp.concatenate([buf[:, s, :] for s in range(groups)], axis=1).astype(xs_ref.dtype)

    @pl.when(i >= te_ref[nt])
    def _():
        xs_ref[...] = jnp.zeros_like(xs_ref)


def _moe_up_kernel(te_ref, xs_ref, wg_ref, wu_ref, h_ref, wg_b, wu_b, *, nt):
    i = pl.program_id(1)

    @pl.when(i < te_ref[nt])
    def _():
        @pl.when((i == 0) | (te_ref[i] != te_ref[jnp.maximum(i - 1, 0)]))
        def _():
            wg_b[...] = wg_ref[...].astype(BF16)
            wu_b[...] = wu_ref[...].astype(BF16)

        a = xs_ref[...]
        gate = jnp.dot(a, wg_b[...], preferred_element_type=F32)
        up = jnp.dot(a, wu_b[...], preferred_element_type=F32)
        h_ref[...] = (_silu(gate) * up).astype(h_ref.dtype)

    @pl.when(i >= te_ref[nt])
    def _():
        h_ref[...] = jnp.zeros_like(h_ref)


def _moe_down_kernel(te_ref, h_ref, w_ref, y_ref, w_b, *, nt):
    i = pl.program_id(1)

    @pl.when(i < te_ref[nt])
    def _():
        @pl.when((i == 0) | (te_ref[i] != te_ref[jnp.maximum(i - 1, 0)]))
        def _():
            w_b[...] = w_ref[...].astype(BF16)

        y_ref[...] = jnp.dot(h_ref[...], w_b[...], preferred_element_type=F32)

    @pl.when(i >= te_ref[nt])
    def _():
        y_ref[...] = jnp.zeros_like(y_ref)


def _moe_combine_kernel(p1_ref, p2_ref, ys_hbm, x_ref, g_ref, info_ref, o_ref, buf1, buf2, sem):
    tm = buf1.shape[0]

    def row_copy(src_row, buf, dst_row):
        return pltpu.make_async_copy(ys_hbm.at[pl.ds(src_row, 1), :], buf.at[pl.ds(dst_row, 1), :], sem)

    def issue(r, carry):
        row_copy(p1_ref[0, r], buf1, r).start(priority=0)
        row_copy(p2_ref[0, r], buf2, r).start(priority=1)
        return carry

    def drain(r, carry):
        row_copy(0, buf1, r).wait()
        row_copy(0, buf2, r).wait()
        return carry

    lax.fori_loop(0, tm, issue, 0, unroll=8)
    lax.fori_loop(0, tm, drain, 0, unroll=8)
    info = info_ref[...]
    y = _lane_col(info, 2) * buf1[...] + _lane_col(info, 3) * buf2[...]
    o_ref[...] = x_ref[...] + g_ref[...] * y


def _moe_ffn(u, info, geo, wg, wu, wd, layer, res_extras):
    t, groups, _ = u.shape
    d = groups * LANES
    _, ne, _, f = wg.shape
    tm = MOE_TM
    nt = (2 * t) // tm + ne
    rows = nt * tm
    rank, cnt = pl.pallas_call(
        _route_rank_kernel, grid=(t // CHUNK,),
        in_specs=[pl.BlockSpec((CHUNK, LANES), lambda i: (i, 0))],
        out_specs=[pl.BlockSpec((CHUNK, LANES), lambda i: (i, 0)), pl.BlockSpec((1, LANES), lambda i: (0, 0))],
        out_shape=[jax.ShapeDtypeStruct((t, LANES), F32), jax.ShapeDtypeStruct((1, LANES), F32)],
        scratch_shapes=[pltpu.VMEM((1, LANES), F32)], name="moe_rank",
        compiler_params=pltpu.CompilerParams(dimension_semantics=("arbitrary",)),
    )(info)
    counts = cnt[0, :ne].astype(jnp.int32)
    gsz = (counts + tm - 1) // tm * tm
    ends = jnp.cumsum(gsz)
    off = ends - gsz
    e1 = info[:, 0].astype(jnp.int32)
    e2 = info[:, 1].astype(jnp.int32)
    pos1 = off[e1] + rank[:, 0].astype(jnp.int32)
    pos2 = off[e2] + rank[:, 1].astype(jnp.int32)
    tok = jnp.arange(t, dtype=jnp.int32)
    inv = jnp.zeros((rows,), jnp.int32).at[jnp.concatenate([pos1, pos2])].set(jnp.concatenate([tok, tok]))
    tile_expert = jnp.sum(jnp.arange(nt, dtype=jnp.int32)[:, None] * tm >= ends[None, :], axis=1)
    te = jnp.concatenate([jnp.minimum(tile_expert, ne - 1), (ends[-1:] // tm)]).astype(jnp.int32)

    smem_row = lambda n: pl.BlockSpec((None, 1, n), lambda i, *_: (i, 0, 0), memory_space=pltpu.SMEM)
    xs = pl.pallas_call(
        functools.partial(_moe_gather_kernel, nt=nt),
        grid_spec=pltpu.PrefetchScalarGridSpec(
            num_scalar_prefetch=1, grid=(nt,),
            in_specs=[smem_row(tm), pl.BlockSpec(memory_space=pl.ANY)],
            out_specs=pl.BlockSpec((tm, d), lambda i, te_ref: (i, 0)),
            scratch_shapes=[pltpu.VMEM((tm, groups, LANES), F32), pltpu.SemaphoreType.DMA(())]),
        out_shape=jax.ShapeDtypeStruct((rows, d), BF16), name="moe_gather",
        compiler_params=pltpu.CompilerParams(dimension_semantics=("arbitrary",)),
    )(te, inv.reshape(nt, 1, tm), u)

    def used(i, te_ref):
        return jnp.minimum(i, te_ref[nt] - 1)

    tn = _pick(f, (512, 256, 128))
    h = pl.pallas_call(
        functools.partial(_moe_up_kernel, nt=nt),
        grid_spec=pltpu.PrefetchScalarGridSpec(
            num_scalar_prefetch=1, grid=(f // tn, nt),
            in_specs=[pl.BlockSpec((tm, d), lambda j, i, te_ref: (used(i, te_ref), 0)),
                      pl.BlockSpec((None, None, d, tn), lambda j, i, te_ref: (layer, te_ref[i], 0, j)),
                      pl.BlockSpec((None, None, d, tn), lambda j, i, te_ref: (layer, te_ref[i], 0, j))],
            out_specs=pl.BlockSpec((tm, tn), lambda j, i, te_ref: (i, j)),
            scratch_shapes=[pltpu.VMEM((d, tn), BF16), pltpu.VMEM((d, tn), BF16)]),
        out_shape=jax.ShapeDtypeStruct((rows, f), BF16), name="moe_up",
        compiler_params=pltpu.CompilerParams(
            dimension_semantics=("arbitrary",) * 2,
            vmem_limit_bytes=_vmem_limit(_nbytes((d, tn), F32) * 2, _nbytes((tm, d), BF16), _nbytes((tm, tn), F32) * 3)),
    )(te, xs, wg, wu)
    tno = 512
    ys = pl.pallas_call(
        functools.partial(_moe_down_kernel, nt=nt),
        grid_spec=pltpu.PrefetchScalarGridSpec(
            num_scalar_prefetch=1, grid=(d // tno, nt),
            in_specs=[pl.BlockSpec((tm, f), lambda j, i, te_ref: (used(i, te_ref), 0)),
                      pl.BlockSpec((None, None, f, tno), lambda j, i, te_ref: (layer, te_ref[i], 0, j))],
            out_specs=pl.BlockSpec((tm, tno), lambda j, i, te_ref: (i, j)),
            scratch_shapes=[pltpu.VMEM((f, tno), BF16)]),
        out_shape=jax.ShapeDtypeStruct((rows, d), F32), name="moe_down",
        compiler_params=pltpu.CompilerParams(
            dimension_semantics=("arbitrary",) * 2,
            vmem_limit_bytes=_vmem_limit(_nbytes((f, tno), F32), _nbytes((f, tno), BF16),
                                         _nbytes((tm, f), BF16), _nbytes((tm, tno), F32))),
    )(te, h, wd)

    x, gate = res_extras
    tc = CHUNK
    cond = _cond_of_block(geo, tc)
    return pl.pallas_call(
        _moe_combine_kernel, grid=(t // tc,),
        in_specs=[smem_row(tc), smem_row(tc), pl.BlockSpec(memory_space=pl.ANY),
                  pl.BlockSpec((tc, d), lambda i: (i, 0)),
                  pl.BlockSpec((None, 1, d), lambda i: (cond(i), 0, 0)),
                  pl.BlockSpec((tc, LANES), lambda i: (i, 0))],
        out_specs=pl.BlockSpec((tc, d), lambda i: (i, 0)),
        out_shape=jax.ShapeDtypeStruct((t, d), F32),
        scratch_shapes=[pltpu.VMEM((tc, d), F32), pltpu.VMEM((tc, d), F32), pltpu.SemaphoreType.DMA(())],
        name="moe_combine",
        compiler_params=pltpu.CompilerParams(dimension_semantics=("arbitrary",)),
    )(pos1.reshape(t // tc, 1, tc), pos2.reshape(t // tc, 1, tc), ys, x, gate, info)


def _trunk(x, conds, geo, state, p):
    depth = p["w_ada"].shape[0]
    d = D_MODEL
    nco = conds.shape[0]
    cpad = jnp.zeros((8, d), F32).at[:nco].set(conds)
    w_ada = p["w_ada"]
    tn = 512
    nj = 6 * d // tn
    (mod,) = _mm(
        [cpad], [w_ada], [(0, 0)], grid=(depth, nj, 1),
        a_specs=[pl.BlockSpec((8, d), lambda l, j, i: (0, 0))],
        b_specs=[pl.BlockSpec((None, d, tn), lambda l, j, i: (l, 0, j))],
        out_shapes=[jax.ShapeDtypeStruct((depth, 8, 6 * d), F32)],
        out_specs=[pl.BlockSpec((None, 8, tn), lambda l, j, i: (l, 0, j))],
        epilogue=lambda accs, ex: (accs[0] + ex[0],), a_fn=_silu,
        extras=[p["b_ada"][:, None, :]],
        extra_specs=[pl.BlockSpec((None, 1, tn), lambda l, j, i: (l, 0, j))], name="ada_mod")
    mod = mod[:, :nco].reshape(depth, nco, 6, 1, d)

    new_states = []
    for i in range(depth):
        sh1, sc1, g1, sh2, sc2, g2 = (mod[i, :, r] for r in range(6))
        u = _norm_mod(x, p["norm_mix_g"][i][None, :], sc1, sh1, geo)
        kind, j = i % 3, i // 3
        res = [x, g1]
        if kind == 0:
            x = _hyena_mixer(u, geo, p["hy_w_in"][j], p["hy_short"][j], p["hy_filt_w1"][j], p["hy_filt_b1"][j],
                             p["hy_filt_w2"][j], p["hy_filt_b2"][j], p["hy_filt_w3"][j], p["hy_filt_bias"][j],
                             p["hy_w_out"][j], res)
        elif kind == 1:
            st = tuple(s[:, j] for s in state)
            x, new_st = _mlstm_mixer(u, geo, st, p["ml_w_up"][j], p["ml_conv"][j], p["ml_w_q"][j],
                                     p["ml_w_k"][j], p["ml_w_v"][j], p["ml_w_gate"][j], p["ml_b_gate"][j],
                                     p["ml_norm_g"][j], p["ml_skip"][j], p["ml_w_down"][j], res)
            new_states.append(new_st)
        else:
            x = _conformer_mixer(u, geo, p["cv_w1"][j], p["cv_b1"][j], p["cv_dw"][j], p["cv_b_dw"][j],
                                 p["cv_ln_g"][j], p["cv_ln_b"][j], p["cv_w2"][j], p["cv_b2"][j], res)
        jf = i // 2
        g_ffn = p["norm_ffn_g"][i][None, :]
        if i % 2 == 0:
            u = _norm_mod(x, g_ffn, sc2, sh2, geo)
            x = _dense_ffn(u, geo, p["ff_w_gate"], p["ff_w_up"], p["ff_w_down"], jf, [x, g2])
        else:
            u, info = _norm_mod_route(x, g_ffn, sc2, sh2, p["moe_router"][jf], geo)
            x = _moe_ffn(u, info, geo, p["moe_w_gate"], p["moe_w_up"], p["moe_w_down"], jf, [x, g2])
    fg = p["final_norm_g"][None, :]
    outs = []
    for nrows, row0 in ((_tp(geo), 0), (_tt(geo) - _tp(geo), _tp(geo))):
        tm = 512
        (y,) = _rowwise(lambda xb, g: (_rms(xb, g),), geo, rows=[(x, d, 0)], params=[(fg, d, 0)],
                        outs=[(d, F32)], tm=tm, n_rows=nrows, row_block0=row0 // tm, name="final_norm")
        outs.append(y)
    new_state = tuple(jnp.stack(parts, axis=1) for parts in zip(*new_states))
    return outs[0], outs[1], new_state


def kernel(x_prompt, x_sample, state_mlstm_C, state_mlstm_n, state_mlstm_m, c, c_ctx, w_ada, b_ada, norm_mix_g, norm_ffn_g, hy_w_in, hy_short, hy_filt_w1, hy_filt_b1, hy_filt_w2, hy_filt_b2, hy_filt_w3, hy_filt_bias, hy_w_out, ml_w_up, ml_conv, ml_w_q, ml_w_k, ml_w_v, ml_w_gate, ml_b_gate, ml_norm_g, ml_skip, ml_w_down, cv_w1, cv_b1, cv_dw, cv_b_dw, cv_ln_g, cv_ln_b, cv_w2, cv_b2, ff_w_gate, ff_w_up, ff_w_down, moe_router, moe_w_gate, moe_w_up, moe_w_down, final_norm_g):
    p = dict(w_ada=w_ada, b_ada=b_ada, norm_mix_g=norm_mix_g, norm_ffn_g=norm_ffn_g, hy_w_in=hy_w_in,
             hy_short=hy_short, hy_filt_w1=hy_filt_w1, hy_filt_b1=hy_filt_b1, hy_filt_w2=hy_filt_w2,
             hy_filt_b2=hy_filt_b2, hy_filt_w3=hy_filt_w3, hy_filt_bias=hy_filt_bias, hy_w_out=hy_w_out,
             ml_w_up=ml_w_up, ml_conv=ml_conv, ml_w_q=ml_w_q, ml_w_k=ml_w_k, ml_w_v=ml_w_v,
             ml_w_gate=ml_w_gate, ml_b_gate=ml_b_gate, ml_norm_g=ml_norm_g, ml_skip=ml_skip,
             ml_w_down=ml_w_down, cv_w1=cv_w1, cv_b1=cv_b1, cv_dw=cv_dw, cv_b_dw=cv_b_dw, cv_ln_g=cv_ln_g,
             cv_ln_b=cv_ln_b, cv_w2=cv_w2, cv_b2=cv_b2, ff_w_gate=ff_w_gate, ff_w_up=ff_w_up,
             ff_w_down=ff_w_down, moe_router=moe_router, moe_w_gate=moe_w_gate, moe_w_up=moe_w_up,
             moe_w_down=moe_w_down, final_norm_g=final_norm_g)
    n_p, l_p, d = x_prompt.shape
    n_s, l_s, _ = x_sample.shape
    geo = Geo(n_p, l_p, n_s, l_s)
    x = jnp.concatenate([x_prompt.reshape(n_p * l_p, d), x_sample.reshape(n_s * l_s, d)], axis=0)
    conds = jnp.concatenate([c_ctx[None, :], c], axis=0)
    y_p, y_s, (c_new, n_new, m_new) = _trunk(
        x, conds, geo, (state_mlstm_C, state_mlstm_n, state_mlstm_m), p)
    n_ml = state_mlstm_C.shape[1]
    return (y_p.reshape(n_p, l_p, d), y_s.reshape(n_s, l_s, d),
            c_new.reshape(n_p, n_ml, 2, ML_HEADS, ML_DH, ML_DH),
            n_new.reshape(n_p, n_ml, 2, ML_HEADS, ML_DH),
            m_new.reshape(n_p, n_ml, 2, ML_HEADS))
```

```python
import collections
import functools
import math

import jax
import jax.numpy as jnp
from jax import lax
from jax.experimental import pallas as pl
from jax.experimental.pallas import tpu as pltpu

F32 = jnp.float32
BF16 = jnp.bfloat16
HIGHEST = lax.Precision.HIGHEST

D_MODEL = 1024
HY_BANDS = 16
HY_SLOW_DECAY = -math.log(1e-2) / 1.5
HY_FAST_DECAY = -math.log(1e-2) / 0.3
HY_SHIFT = 0.05
ML_HEADS = 8
ML_DH = 256
N_EXPERTS = 8
EPS = 1e-6

LANES = 128
SUBLANES = 8
CHUNK = 256
CONV_HALO = 16
CONV_ROWS = 64
MOE_TM = 1024
VMEM_CAP = 56 * 1024 * 1024
COL_TILE_BUDGET = 40 * 1024 * 1024

Geo = collections.namedtuple("Geo", "n_p l_p n_s l_s")


def _tp(geo):
    return geo.n_p * geo.l_p


def _tt(geo):
    return geo.n_p * geo.l_p + geo.n_s * geo.l_s


def _cond_of_block(geo, rows):
    tp = _tp(geo)

    def f(i):
        r0 = i * rows
        return jnp.where(r0 < tp, 0, 1 + (r0 - tp) // geo.l_s)

    return f


def _vmem_limit(*nbytes):
    est = 2 * sum(nbytes) + (8 << 20)
    return int(min(max(est, 32 << 20), VMEM_CAP))


def _nbytes(shape, dtype):
    n = 1
    for s in shape:
        if s is not None:
            n *= s
    return n * jnp.dtype(dtype).itemsize


def _silu(x):
    return x * (1.0 / (1.0 + jnp.exp(-x)))


def _sigmoid(x):
    return 1.0 / (1.0 + jnp.exp(-x))


def _mm_kernel(*refs, na, nb, ne, no, pairs, cast_b, epilogue, a_fn, inner_axis):
    a_refs = refs[:na]
    b_refs = refs[na:na + nb]
    e_refs = refs[na + nb:na + nb + ne]
    o_refs = refs[na + nb + ne:na + nb + ne + no]
    scratch = refs[na + nb + ne + no:]
    first = pl.program_id(inner_axis) == 0
    b_src = []
    si = 0
    for j in range(nb):
        if cast_b[j]:
            s_ref = scratch[si]
            si += 1

            @pl.when(first)
            def _(s_ref=s_ref, b_ref=b_refs[j]):
                s_ref[...] = b_ref[...].astype(BF16)

            b_src.append(s_ref)
        else:
            b_src.append(b_refs[j])
    a_vals = {}
    accs = []
    for (i, j) in pairs:
        if i not in a_vals:
            a = a_refs[i][...]
            if a_fn is not None:
                a = a_fn(a)
            a_vals[i] = a.astype(BF16)
        accs.append(jnp.dot(a_vals[i], b_src[j][...], preferred_element_type=F32))
    outs = epilogue(accs, [e[...] for e in e_refs])
    for o_ref, val in zip(o_refs, outs):
        o_ref[...] = val.astype(o_ref.dtype)


def _mm(a_list, b_list, pairs, *, grid, a_specs, b_specs, out_shapes, out_specs,
        epilogue, extras=(), extra_specs=(), a_fn=None, name="mm"):
    cast_b = tuple(b.dtype != BF16 for b in b_list)
    scratch = [pltpu.VMEM(tuple(s for s in spec.block_shape if s is not None), BF16)
               for b, spec, c in zip(b_list, b_specs, cast_b) if c]
    kern = functools.partial(
        _mm_kernel, na=len(a_list), nb=len(b_list), ne=len(extras), no=len(out_shapes),
        pairs=tuple(pairs), cast_b=cast_b, epilogue=epilogue, a_fn=a_fn, inner_axis=len(grid) - 1)
    sizes = [_nbytes(s.block_shape, a.dtype) for a, s in zip(a_list, a_specs)]
    sizes += [_nbytes(s.block_shape, b.dtype) for b, s in zip(b_list, b_specs)]
    sizes += [_nbytes(s.block_shape, e.dtype) for e, s in zip(extras, extra_specs)]
    sizes += [_nbytes(s.block_shape, o.dtype) for o, s in zip(out_shapes, out_specs)]
    sizes += [_nbytes(s.block_shape, BF16) // 2 for s, c in zip(b_specs, cast_b) if c]
    res = pl.pallas_call(
        kern, grid=grid,
        in_specs=list(a_specs) + list(b_specs) + list(extra_specs),
        out_specs=list(out_specs), out_shape=list(out_shapes),
        scratch_shapes=scratch, name=name,
        compiler_params=pltpu.CompilerParams(
            dimension_semantics=("arbitrary",) * len(grid), vmem_limit_bytes=_vmem_limit(*sizes)),
    )(*a_list, *b_list, *extras)
    return res


def _pick(n, prefs):
    for p in prefs:
        if n % p == 0:
            return p
    return n


def _row_tile(m, geo):
    base = math.gcd(m, geo.l_s, _tp(geo) or m) if geo is not None else m
    return _pick(base, (1024, 512, 256, 128, 8))


def _col_tile(k, n_out, n_w, tm, out_bytes):
    for tn in (1408, 1024, 512, 256):
        need = n_w * k * tn * (2 * 4 + 2) + 2 * tm * k * 2 + 2 * tm * tn * (out_bytes + 4)
        if n_out % tn == 0 and need <= COL_TILE_BUDGET:
            return tn
    return LANES


def _linear(a, w, *, out_dtype, epilogue=None, extras=(), extra_specs_fn=None, tm=None, tn=None,
            a_col=0, w_col=0, w_row=0, n_out=None, a_fn=None, k=None, geo=None, name="linear"):
    m = a.shape[0]
    k = k or w.shape[0]
    n_out = n_out or w.shape[1]
    tm = tm or _row_tile(m, geo)
    tn = tn or _col_tile(k, n_out, 1, tm, jnp.dtype(out_dtype).itemsize)
    grid = (n_out // tn, m // tm)
    jo = w_col * (n_out // tn)
    especs = extra_specs_fn(lambda j, i: i, lambda j, i: j, tm, tn) if extras else ()
    ep = epilogue or (lambda accs, ex: (accs[0],))
    (out,) = _mm(
        [a], [w], [(0, 0)], grid=grid,
        a_specs=[pl.BlockSpec((tm, k), lambda j, i: (i, a_col))],
        b_specs=[pl.BlockSpec((k, tn), lambda j, i: (w_row, jo + j))],
        out_shapes=[jax.ShapeDtypeStruct((m, n_out), out_dtype)],
        out_specs=[pl.BlockSpec((tm, tn), lambda j, i: (i, j))],
        epilogue=ep, extras=extras, extra_specs=especs, a_fn=a_fn, name=name)
    return out


def _residual_specs(geo):
    def fn(i_of, j_of, tm, tn):
        cond = _cond_of_block(geo, tm)
        return [pl.BlockSpec((tm, tn), lambda *g: (i_of(*g), j_of(*g))),
                pl.BlockSpec((None, 1, tn), lambda *g: (cond(i_of(*g)), 0, j_of(*g)))]
    return fn


def _residual_bias_specs(geo):
    base = _residual_specs(geo)

    def fn(i_of, j_of, tm, tn):
        return base(i_of, j_of, tm, tn) + [pl.BlockSpec((1, tn), lambda *g: (0, j_of(*g)))]
    return fn


def _ep_residual(accs, ex):
    return (ex[0] + ex[1] * accs[0],)


def _ep_residual_bias(accs, ex):
    return (ex[0] + ex[1] * (accs[0] + ex[2]),)


def _dual_linear(a, w1, w2, *, epilogue, out_dtype, n_out, w1_col=0, w2_col=0, extras=(),
                 extra_specs_fn=None, tm=None, tn=None, geo=None, w_row=0, name="dual_linear"):
    m, k = a.shape
    tm = tm or _row_tile(m, geo)
    tn = tn or _col_tile(k, n_out, 2, tm, jnp.dtype(out_dtype).itemsize)
    nj = n_out // tn
    grid = (nj, m // tm)
    especs = extra_specs_fn(lambda j, i: i, lambda j, i: j, tm, tn) if extras else ()
    (out,) = _mm(
        [a], [w1, w2], [(0, 0), (0, 1)], grid=grid,
        a_specs=[pl.BlockSpec((tm, k), lambda j, i: (i, 0))],
        b_specs=[pl.BlockSpec((k, tn), lambda j, i: (w_row, w1_col * nj + j)),
                 pl.BlockSpec((k, tn), lambda j, i: (w_row, w2_col * nj + j))],
        out_shapes=[jax.ShapeDtypeStruct((m, n_out), out_dtype)],
        out_specs=[pl.BlockSpec((tm, tn), lambda j, i: (i, j))],
        epilogue=epilogue, extras=extras, extra_specs=especs, name=name)
    return out


def _rowwise_kernel(*refs, nin, fn):
    ins = [r[...] for r in refs[:nin]]
    outs = fn(*ins)
    for o_ref, val in zip(refs[nin:], outs):
        o_ref[...] = val.astype(o_ref.dtype)


def _rowwise(fn, geo, *, rows=(), params=(), conds=(), outs, tm, n_rows=None, row_block0=0, name="rowwise"):
    n_rows = n_rows or rows[0][0].shape[0]
    cond = _cond_of_block(geo, tm)
    in_specs, args, sizes = [], [], []
    for arr, w, cb in rows:
        in_specs.append(pl.BlockSpec((tm, w), lambda i, cb=cb: (i + row_block0, cb)))
        args.append(arr)
        sizes.append(_nbytes((tm, w), arr.dtype))
    for arr, w, cb in params:
        in_specs.append(pl.BlockSpec((1, w), lambda i, cb=cb: (0, cb)))
        args.append(arr)
    for arr in conds:
        w = arr.shape[-1]
        in_specs.append(pl.BlockSpec((None, 1, w), lambda i: (cond(i + row_block0), 0, 0)))
        args.append(arr)
    out_shapes = [jax.ShapeDtypeStruct((n_rows, w), dt) for w, dt in outs]
    out_specs = [pl.BlockSpec((tm, w), lambda i: (i, 0)) for w, dt in outs]
    sizes += [_nbytes((tm, w), dt) for w, dt in outs]
    return pl.pallas_call(
        functools.partial(_rowwise_kernel, nin=len(args), fn=fn),
        grid=(n_rows // tm,), in_specs=in_specs, out_specs=out_specs, out_shape=out_shapes, name=name,
        compiler_params=pltpu.CompilerParams(
            dimension_semantics=("arbitrary",), vmem_limit_bytes=_vmem_limit(*sizes, *sizes)),
    )(*args)


def _rms(x, g):
    return x * lax.rsqrt(jnp.mean(x * x, axis=-1, keepdims=True) + EPS) * g


def _norm_mod_fn(x, g, sc, sh):
    return (_rms(x, g) * (1.0 + sc) + sh,)


def _norm_mod(x, g, sc, sh, geo):
    (u,) = _rowwise(_norm_mod_fn, geo, rows=[(x, D_MODEL, 0)], params=[(g, D_MODEL, 0)],
                    conds=[sc, sh], outs=[(D_MODEL, BF16)], tm=512, name="norm_mod")
    return u


def _norm_mod_route_fn(x, g, router, sc, sh):
    u = _rms(x, g) * (1.0 + sc) + sh
    logits = jnp.dot(u, router, preferred_element_type=F32, precision=HIGHEST)
    lane = lax.broadcasted_iota(jnp.int32, logits.shape, 1)
    neg = jnp.float32(-jnp.inf)
    logits = jnp.where(lane < N_EXPERTS, logits, neg)
    v1 = jnp.max(logits, axis=-1, keepdims=True)
    i1 = jnp.min(jnp.where(logits == v1, lane, LANES), axis=-1, keepdims=True)
    rest = jnp.where(lane == i1, neg, logits)
    v2 = jnp.max(rest, axis=-1, keepdims=True)
    i2 = jnp.min(jnp.where(rest == v2, lane, LANES), axis=-1, keepdims=True)
    e2 = jnp.exp(v2 - v1)
    p1 = 1.0 / (1.0 + e2)
    p2 = e2 / (1.0 + e2)
    info = jnp.where(lane == 0, i1.astype(F32),
                     jnp.where(lane == 1, i2.astype(F32),
                               jnp.where(lane == 2, p1, jnp.where(lane == 3, p2, 0.0))))
    return u, info


def _norm_mod_route_kernel(x_ref, g_ref, r_ref, sc_ref, sh_ref, u_ref, info_ref):
    u, info = _norm_mod_route_fn(x_ref[...], g_ref[...], r_ref[...], sc_ref[...], sh_ref[...])
    for s in range(u_ref.shape[1]):
        u_ref[:, s, :] = u[:, s * LANES:(s + 1) * LANES]
    info_ref[...] = info


def _norm_mod_route(x, g, sc, sh, router, geo):
    tm = 256
    cond = _cond_of_block(geo, tm)
    n = x.shape[0]
    router_p = jnp.zeros((D_MODEL, LANES), F32).at[:, :N_EXPERTS].set(router)
    row = pl.BlockSpec((tm, D_MODEL), lambda i: (i, 0))
    cspec = pl.BlockSpec((None, 1, D_MODEL), lambda i: (cond(i), 0, 0))
    groups = D_MODEL // LANES
    return pl.pallas_call(
        _norm_mod_route_kernel,
        grid=(n // tm,),
        in_specs=[row, pl.BlockSpec((1, D_MODEL), lambda i: (0, 0)),
                  pl.BlockSpec((D_MODEL, LANES), lambda i: (0, 0)), cspec, cspec],
        out_specs=[pl.BlockSpec((tm, groups, LANES), lambda i: (i, 0, 0)),
                   pl.BlockSpec((tm, LANES), lambda i: (i, 0))],
        out_shape=[jax.ShapeDtypeStruct((n, groups, LANES), F32), jax.ShapeDtypeStruct((n, LANES), F32)],
        name="norm_mod_route", compiler_params=pltpu.CompilerParams(dimension_semantics=("arbitrary",)),
    )(x, g, router_p, sc, sh)


def _dwconv_kernel(x_ref, w_ref, *rest, taps, geo, rb, post, has_bias):
    if has_bias:
        b_ref, o_ref, pad_ref = rest
    else:
        o_ref, pad_ref = rest
    half = (taps - 1) // 2
    tc = x_ref.shape[1]
    n_prompt_blocks = _tp(geo) // rb
    first = CONV_HALO - half
    win0 = first // SUBLANES * SUBLANES
    shift0 = first - win0
    win_rows = -(-(shift0 + taps - 1 + CONV_ROWS) // SUBLANES) * SUBLANES

    def run(nseq, ln):
        for s in range(nseq):
            base = s * ln
            zeros = jnp.zeros((CONV_HALO, tc), F32)
            pad_ref[0:CONV_HALO, :] = zeros
            pad_ref[CONV_HALO + ln:2 * CONV_HALO + ln, :] = zeros
            pad_ref[CONV_HALO:CONV_HALO + ln, :] = x_ref[base:base + ln, :]
            for r0 in range(0, ln, CONV_ROWS):
                acc = jnp.zeros((CONV_ROWS, tc), F32)
                win = pad_ref[win0 + r0:win0 + r0 + win_rows, :]
                for r in range(SUBLANES):
                    group = [kk for kk in range(taps) if (shift0 + kk) % SUBLANES == r]
                    if not group:
                        continue
                    rot = win if r == 0 else pltpu.roll(win, win_rows - r, 0)
                    for kk in group:
                        m0 = (shift0 + kk) // SUBLANES * SUBLANES
                        acc = acc + w_ref[kk:kk + 1, :] * rot[m0:m0 + CONV_ROWS]
                if has_bias:
                    acc = acc + b_ref[...]
                o_ref[base + r0:base + r0 + CONV_ROWS, :] = post(acc)

    i = pl.program_id(0)
    if n_prompt_blocks > 0:
        @pl.when(i < n_prompt_blocks)
        def _():
            run(rb // geo.l_p, geo.l_p)

    @pl.when(i >= n_prompt_blocks)
    def _():
        run(1, geo.l_s)


def _dwconv(x, w, geo, *, n_ch, bias=None, post=None, tc=256):
    taps = w.shape[0]
    rb = geo.l_s
    t = x.shape[0]
    post = post or (lambda v: v)
    in_specs = [pl.BlockSpec((rb, tc), lambda i, j: (i, j)), pl.BlockSpec((taps, tc), lambda i, j: (0, j))]
    args = [x, w]
    if bias is not None:
        in_specs.append(pl.BlockSpec((1, tc), lambda i, j: (0, j)))
        args.append(bias)
    blk = _nbytes((rb, tc), F32)
    return pl.pallas_call(
        functools.partial(_dwconv_kernel, taps=taps, geo=geo, rb=rb, post=post, has_bias=bias is not None),
        grid=(t // rb, n_ch // tc), in_specs=in_specs,
        out_specs=pl.BlockSpec((rb, tc), lambda i, j: (i, j)),
        out_shape=jax.ShapeDtypeStruct((t, n_ch), F32),
        scratch_shapes=[pltpu.VMEM((rb + 2 * CONV_HALO, tc), F32)], name="dwconv%d" % taps,
        compiler_params=pltpu.CompilerParams(
            dimension_semantics=("arbitrary", "arbitrary"), vmem_limit_bytes=_vmem_limit(blk, blk, blk)),
    )(*args)


def _dft_tables(ln):
    kb = 64
    c3 = jnp.arange(ln, dtype=jnp.int32)[None, None, :]
    a3 = jnp.arange(ln // kb, dtype=jnp.int32)[:, None, None]
    b3 = jnp.arange(kb, dtype=jnp.int32)[None, :, None]
    ang_a = ((kb * a3 * c3) % (2 * ln)).astype(F32) * (math.pi / ln)
    ang_b = ((b3 * c3) % (2 * ln)).astype(F32) * (math.pi / ln)
    ca, sa, cb, sb = jnp.cos(ang_a), jnp.sin(ang_a), jnp.cos(ang_b), jnp.sin(ang_b)
    cos = ca * cb - sa * sb
    sin = sa * cb + ca * sb
    row0 = (a3 == 0) & (b3 == 0)
    col0 = c3 == 0
    alt_col = jnp.where(c3 % 2 == 0, 1.0, -1.0).astype(F32)
    alt_row = jnp.where(b3 % 2 == 0, 1.0, -1.0).astype(F32)
    half = jnp.arange(2, dtype=jnp.int32)[:, None, None, None]
    fwd = jnp.where(half == 0, cos[None], jnp.where(row0, alt_col, -sin)[None])
    fwd = fwd.reshape(2 * ln, ln).astype(BF16)
    scale = 1.0 / (2 * ln)
    inv_c = (jnp.where(col0, 1.0, 2.0 * cos) * scale).reshape(ln, ln).astype(BF16)
    inv_s = (jnp.where(col0, alt_row, -2.0 * sin) * scale).reshape(ln, ln).astype(BF16)
    return fwd, inv_c, inv_s


def _filter_hidden_kernel(w1_ref, b1_ref, w2_ref, b2_ref, o_ref, *, ln):
    pos = lax.broadcasted_iota(jnp.int32, (ln, LANES), 0).astype(F32) / ln
    lane = lax.broadcasted_iota(jnp.int32, (ln, LANES), 1)
    band = jnp.where(lane <= HY_BANDS, lane, lane - HY_BANDS).astype(F32)
    ang = (2.0 * math.pi) * pos * band
    feats = jnp.where(lane == 0, pos,
                      jnp.where(lane <= HY_BANDS, jnp.cos(ang),
                                jnp.where(lane <= 2 * HY_BANDS, jnp.sin(ang), 0.0)))
    h = jnp.sin(jnp.dot(feats, w1_ref[...], preferred_element_type=F32, precision=HIGHEST) + b1_ref[...])
    o_ref[...] = jnp.sin(jnp.dot(h, w2_ref[...], preferred_element_type=F32, precision=HIGHEST) + b2_ref[...])


def _filter_kernel(h_ref, w3_ref, hw_ref, ss_ref, *, ln):
    tn = hw_ref.shape[1]
    j = pl.program_id(0)
    h = jnp.dot(h_ref[...], w3_ref[...], preferred_element_type=F32, precision=HIGHEST)
    ch = (j * tn) % D_MODEL + lax.broadcasted_iota(jnp.int32, (1, tn), 1)
    step = (HY_FAST_DECAY - HY_SLOW_DECAY) / (D_MODEL - 1)
    deltas = HY_SLOW_DECAY + ch.astype(F32) * step
    tcol = lax.broadcasted_iota(jnp.int32, (ln, 1), 0).astype(F32) / ln
    h = h * (jnp.exp(-tcol * deltas) + HY_SHIFT)
    hw_ref[...] = h
    ss_ref[...] = jnp.sum(h * h, axis=0, keepdims=True)


def _hyena_filters(ln, w1, b1, w2, b2, w3, fwd):
    hid = w1.shape[1]
    ncol = w3.shape[1]
    tn = 512
    w1p = jnp.zeros((LANES, hid), F32).at[:w1.shape[0]].set(w1)
    full = lambda shape: pl.BlockSpec(shape, lambda j: (0, 0))
    hidden = pl.pallas_call(
        functools.partial(_filter_hidden_kernel, ln=ln),
        grid=(1,),
        in_specs=[full((LANES, hid)), full((1, hid)), full((hid, hid)), full((1, hid))],
        out_specs=full((ln, hid)),
        out_shape=jax.ShapeDtypeStruct((ln, hid), F32),
        name="hy_filter_hidden",
        compiler_params=pltpu.CompilerParams(dimension_semantics=("arbitrary",)),
    )(w1p, b1[None, :], w2, b2[None, :])
    hw, ss = pl.pallas_call(
        functools.partial(_filter_kernel, ln=ln),
        grid=(ncol // tn,),
        in_specs=[full((ln, hid)), pl.BlockSpec((hid, tn), lambda j: (0, j))],
        out_specs=[pl.BlockSpec((ln, tn), lambda j: (0, j)), pl.BlockSpec((1, tn), lambda j: (0, j))],
        out_shape=[jax.ShapeDtypeStruct((ln, ncol), F32), jax.ShapeDtypeStruct((1, ncol), F32)],
        name="hy_filter_cols",
        compiler_params=pltpu.CompilerParams(dimension_semantics=("arbitrary",),
                                             vmem_limit_bytes=_vmem_limit(_nbytes((ln, tn), F32) * 4)),
    )(hidden, w3)
    spec = _linear(fwd, hw, out_dtype=F32, tm=_pick(2 * ln, (1024, 512)), name="hy_filter_dft")
    half = ncol // 2
    tk = _pick(ln, (512, 256))

    def assemble(fc, fs, bc, bs, ssf, ssb):
        scale = lax.rsqrt(ssf + ssb + EPS)
        row0 = (pl.program_id(0) == 0) & (lax.broadcasted_iota(jnp.int32, fc.shape, 0) == 0)
        gc = (fc + bc) * scale
        gs = jnp.where(row0, fs + bs, fs - bs) * scale
        return gc, gs

    nk = ln // tk
    nj = half // tn
    return pl.pallas_call(
        functools.partial(_rowwise_kernel, nin=6, fn=assemble),
        grid=(nk, nj),
        in_specs=[pl.BlockSpec((tk, tn), lambda i, j: (i, j)),
                  pl.BlockSpec((tk, tn), lambda i, j: (nk + i, j)),
                  pl.BlockSpec((tk, tn), lambda i, j: (i, nj + j)),
                  pl.BlockSpec((tk, tn), lambda i, j: (nk + i, nj + j)),
                  pl.BlockSpec((1, tn), lambda i, j: (0, j)),
                  pl.BlockSpec((1, tn), lambda i, j: (0, nj + j))],
        out_specs=[pl.BlockSpec((tk, tn), lambda i, j: (i, j)), pl.BlockSpec((tk, tn), lambda i, j: (i, j))],
        out_shape=[jax.ShapeDtypeStruct((ln, half), F32)] * 2, name="hy_filter_assemble",
        compiler_params=pltpu.CompilerParams(dimension_semantics=("arbitrary", "arbitrary")),
    )(spec, spec, spec, spec, ss, ss)


def _dft_fwd_kernel(fc_ref, fs_ref, z_ref, gc_ref, gs_ref, yc_ref, ys_ref, zb_ref, *, nb):
    @pl.when(pl.program_id(2) == 0)
    def _():
        zb_ref[...] = z_ref[...].astype(BF16)

    fc = fc_ref[...]
    fs = fs_ref[...]
    gcb = gc_ref[...]
    gsb = gs_ref[...]
    row_0 = (pl.program_id(2) == 0) & (lax.broadcasted_iota(jnp.int32, gcb.shape, 0) == 0)
    for s in range(nb):
        zc = jnp.dot(fc, zb_ref[s], preferred_element_type=F32)
        zs = jnp.dot(fs, zb_ref[s], preferred_element_type=F32)
        yc_ref[s] = jnp.where(row_0, zc * gcb, zc * gcb - zs * gsb).astype(yc_ref.dtype)
        ys_ref[s] = jnp.where(row_0, zs * gsb, zc * gsb + zs * gcb).astype(ys_ref.dtype)


def _dft_inv_kernel(ic_ref, is_ref, yc_ref, ys_ref, zin_ref, gate_ref, bias_ref, o_ref, *, nb):
    ic = ic_ref[...]
    isn = is_ref[...]
    bias = bias_ref[...]
    for s in range(nb):
        acc = jnp.dot(ic, yc_ref[s], preferred_element_type=F32)
        acc = acc + jnp.dot(isn, ys_ref[s], preferred_element_type=F32)
        o_ref[s] = (gate_ref[s] * (acc + zin_ref[s] * bias)).astype(o_ref.dtype)


def _long_conv_gated(z, z_col, pc, gate_col, gc, gs, order, bias, tables, *, ln, nseq, row0, out_dtype):
    fwd, inv_c, inv_s = tables
    d = D_MODEL
    tn = 512
    nj = d // tn
    tk = _pick(ln, (512, 256))
    nk = ln // tk
    nb = max(n for n in (8, 4, 2, 1) if nseq % n == 0 and n * ln <= 2048)
    z3 = z.reshape(z.shape[0] // ln, ln, z.shape[1])
    pc3 = pc.reshape(pc.shape[0] // ln, ln, pc.shape[1])
    z_sb0 = (row0 // ln if z.shape[0] != nseq * ln else 0) // nb
    pc_sb0 = (row0 // ln) // nb
    seq_blk = (nb, ln, tn)
    out_blk = (nb, tk, tn)
    cparams = pltpu.CompilerParams(
        dimension_semantics=("arbitrary",) * 3,
        vmem_limit_bytes=_vmem_limit(_nbytes(seq_blk, F32) * 2, _nbytes(out_blk, F32) * 3, _nbytes((tk, ln), BF16) * 2))
    yc, ys = pl.pallas_call(
        functools.partial(_dft_fwd_kernel, nb=nb), grid=(nseq // nb, nj, nk),
        in_specs=[pl.BlockSpec((tk, ln), lambda b, j, i: (i, 0)),
                  pl.BlockSpec((tk, ln), lambda b, j, i: (nk + i, 0)),
                  pl.BlockSpec(seq_blk, lambda b, j, i: (z_sb0 + b, 0, z_col * nj + j)),
                  pl.BlockSpec((tk, tn), lambda b, j, i: (i, order * nj + j)),
                  pl.BlockSpec((tk, tn), lambda b, j, i: (i, order * nj + j))],
        out_specs=[pl.BlockSpec(out_blk, lambda b, j, i: (b, i, j))] * 2,
        out_shape=[jax.ShapeDtypeStruct((nseq, ln, d), BF16)] * 2,
        scratch_shapes=[pltpu.VMEM(seq_blk, BF16)], name="hy_dft_fwd", compiler_params=cparams,
    )(fwd, fwd, z3, gc, gs)
    out = pl.pallas_call(
        functools.partial(_dft_inv_kernel, nb=nb), grid=(nseq // nb, nj, nk),
        in_specs=[pl.BlockSpec((tk, ln), lambda b, j, i: (i, 0)),
                  pl.BlockSpec((tk, ln), lambda b, j, i: (i, 0)),
                  pl.BlockSpec(seq_blk, lambda b, j, i: (b, 0, j)),
                  pl.BlockSpec(seq_blk, lambda b, j, i: (b, 0, j)),
                  pl.BlockSpec(out_blk, lambda b, j, i: (z_sb0 + b, i, z_col * nj + j)),
                  pl.BlockSpec(out_blk, lambda b, j, i: (pc_sb0 + b, i, gate_col * nj + j)),
                  pl.BlockSpec((None, 1, tn), lambda b, j, i: (order, 0, j))],
        out_specs=pl.BlockSpec(out_blk, lambda b, j, i: (b, i, j)),
        out_shape=jax.ShapeDtypeStruct((nseq, ln, d), out_dtype),
        name="hy_dft_inv", compiler_params=cparams,
    )(inv_c, inv_s, yc, ys, z3, pc3, bias[:, None, :])
    return out.reshape(nseq * ln, d)


def _hyena_in_conv(u, w_in, short, geo):
    assert short.shape[0] == 3
    rb = geo.l_s
    assert rb % geo.l_p == 0 and geo.l_p & (geo.l_p - 1) == 0 and rb & (rb - 1) == 0
    n_prompt_tiles = _tp(geo) // rb

    def conv3(accs, ex):
        acc, w = accs[0], ex[0]
        seq = jnp.where(pl.program_id(1) < n_prompt_tiles, geo.l_p, geo.l_s)
        pos = lax.broadcasted_iota(jnp.int32, acc.shape, 0) & (seq - 1)
        prev = jnp.where(pos == 0, 0.0, pltpu.roll(acc, 1, 0))
        nxt = jnp.where(pos == seq - 1, 0.0, pltpu.roll(acc, rb - 1, 0))
        return (w[0:1] * prev + w[1:2] * acc + w[2:3] * nxt,)

    return _linear(u, w_in, out_dtype=F32, tm=rb, epilogue=conv3, extras=[short],
                   extra_specs_fn=lambda i_of, j_of, tm, tn: [pl.BlockSpec((3, tn), lambda *g: (0, j_of(*g)))],
                   name="hy_in_conv")


def _hyena_mixer(u, geo, w_in, short, w1, b1, w2, b2, w3, filt_bias, w_out, res_extras):
    d = D_MODEL
    pc = _hyena_in_conv(u, w_in, short, geo)
    parts = []
    for ln, nseq, row0 in ((geo.l_p, geo.n_p, 0), (geo.l_s, geo.n_s, _tp(geo))):
        if nseq == 0:
            continue
        tables = _dft_tables(ln)
        gc, gs = _hyena_filters(ln, w1, b1, w2, b2, w3, tables[0])
        z1 = _long_conv_gated(pc, 0, pc, 1, gc, gs, 0, filt_bias, tables, ln=ln, nseq=nseq, row0=row0,
                              out_dtype=F32)
        z2 = _long_conv_gated(z1, 0, pc, 2, gc, gs, 1, filt_bias, tables, ln=ln, nseq=nseq, row0=row0,
                              out_dtype=BF16)
        parts.append(z2)
    z = jnp.concatenate(parts, axis=0) if len(parts) > 1 else parts[0]
    return _linear(z, w_out, out_dtype=F32, epilogue=_ep_residual, extras=res_extras,
                   extra_specs_fn=_residual_specs(geo), geo=geo, name="hy_out")


def _head_linear(a, ws, scales):
    t = a.shape[0]
    tm = _pick(t, (1024, 512, 256))
    nw = len(ws)
    return _mm(
        [a], [w.reshape(ML_HEADS * ML_DH, ML_DH) for w in ws], [(0, n) for n in range(nw)],
        grid=(ML_HEADS, t // tm),
        a_specs=[pl.BlockSpec((tm, ML_DH), lambda h, i: (i, h))],
        b_specs=[pl.BlockSpec((ML_DH, ML_DH), lambda h, i: (h, 0))] * nw,
        out_shapes=[jax.ShapeDtypeStruct((t, ML_HEADS * ML_DH), BF16)] * nw,
        out_specs=[pl.BlockSpec((tm, ML_DH), lambda h, i: (i, h))] * nw,
        epilogue=lambda accs, ex: tuple(acc if s is None else acc * s for acc, s in zip(accs, scales)),
        name="ml_head_linear")


def _gate_prep_kernel(pre_ref, o_ref):
    pre = pre_ref[...]
    lane = lax.broadcasted_iota(jnp.int32, pre.shape, 1)
    is_f = (lane % 16) >= ML_HEADS
    lf = jnp.minimum(pre, 0.0) - jnp.log(1.0 + jnp.exp(-jnp.abs(pre)))
    r = lax.broadcasted_iota(jnp.int32, (CHUNK, CHUNK), 0)
    c = lax.broadcasted_iota(jnp.int32, (CHUNK, CHUNK), 1)
    lower = (c <= r).astype(F32)
    upper = (c >= r).astype(F32)
    pre_sum = jnp.dot(lower, lf, preferred_element_type=F32, precision=HIGHEST)
    suf_sum = jnp.dot(upper, lf, preferred_element_type=F32, precision=HIGHEST)
    cum = jnp.where(lane < 16, pre_sum, suf_sum)
    o_ref[...] = jnp.where(is_f, cum, pre)


def _mlstm_scan_kernel(*refs, has_init, nc):
    if has_init:
        q_ref, k_ref, v_ref, gc_ref, gr_ref, c0_ref, n0_ref, m0_ref, h_ref, c_ref, n_ref, m_ref = refs
    else:
        q_ref, k_ref, v_ref, gc_ref, gr_ref, h_ref, c_ref, n_ref, m_ref = refs
    dr = pl.program_id(0)
    ci = pl.program_id(2)

    carry = has_init or nc > 1
    if carry:
        @pl.when(ci == 0)
        def _():
            if has_init:
                c_ref[...] = c0_ref[...]
                n_ref[...] = n0_ref[...]
                m_ref[...] = m0_ref[...]
            else:
                c_ref[...] = jnp.zeros_like(c_ref)
                n_ref[...] = jnp.zeros_like(n_ref)
                m_ref[...] = jnp.zeros_like(m_ref)

    gcv = gc_ref[...]
    lane = lax.broadcasted_iota(jnp.int32, gcv.shape, 1)
    rowi = lax.broadcasted_iota(jnp.int32, (CHUNK, CHUNK), 0)
    coli = lax.broadcasted_iota(jnp.int32, (CHUNK, CHUNK), 1)
    mask = (rowi - coli) * (1 - 2 * dr) >= 0
    for hh in range(ML_HEADS):
        cols = slice(hh * ML_DH, (hh + 1) * ML_DH)
        q = q_ref[:, cols]
        k = k_ref[:, cols]
        v = v_ref[:, cols]
        li_idx = dr * 16 + hh
        b_idx = li_idx + ML_HEADS
        bcol = jnp.sum(jnp.where(lane == b_idx, gcv, 0.0), axis=1, keepdims=True)
        licol = jnp.sum(jnp.where(lane == li_idx, gcv, 0.0), axis=1, keepdims=True)
        brow = gr_ref[pl.ds(b_idx, 1), :]
        lirow = gr_ref[pl.ds(li_idx, 1), :]
        m_prev = m_ref[hh] if carry else jnp.zeros((1, 1), F32)

        dmat = jnp.where(mask, bcol - brow + lirow, -jnp.inf)
        a = bcol + m_prev
        mt = jnp.maximum(a, jnp.max(dmat, axis=1, keepdims=True))
        qk = lax.dot_general(q, k, (((1,), (1,)), ((), ())), preferred_element_type=F32)
        s = qk * jnp.exp(dmat - mt)
        num = jnp.dot(s.astype(BF16), v, preferred_element_type=F32)
        den = jnp.sum(s, axis=1, keepdims=True)
        if carry:
            w_inter = jnp.exp(a - mt)
            c_prev = c_ref[hh]
            n_prev = n_ref[hh]
            num = num + w_inter * jnp.dot(q, c_prev.astype(BF16), preferred_element_type=F32)
            den = den + w_inter * jnp.sum(q.astype(F32) * n_prev, axis=1, keepdims=True)
        h_ref[:, cols] = num / jnp.maximum(jnp.abs(den), jnp.exp(-mt))

        b_last = jnp.min(bcol, axis=0, keepdims=True)
        dl = b_last - bcol + licol
        m_new = jnp.maximum(b_last + m_prev, jnp.max(dl, axis=0, keepdims=True))
        kw = k.astype(F32) * jnp.exp(dl - m_new)
        c_new = lax.dot_general(kw.astype(BF16), v, (((0,), (0,)), ((), ())), preferred_element_type=F32)
        n_new = jnp.sum(kw, axis=0, keepdims=True)
        if carry:
            w_old = jnp.exp(b_last + m_prev - m_new)
            c_new = w_old * c_prev + c_new
            n_new = w_old * n_prev + n_new
        c_ref[hh] = c_new
        n_ref[hh] = n_new
        m_ref[hh] = m_new


def _mlstm_scan(q, k, v, gcols, grows, *, ln, nseq, row0, init=None):
    nc = ln // CHUNK
    rb0 = row0 // CHUNK
    hd = ML_HEADS * ML_DH

    def blk(dr, b, c):
        return rb0 + b * nc + c + dr * (nc - 1 - 2 * c)

    row_spec = pl.BlockSpec((CHUNK, hd), lambda dr, b, c: (blk(dr, b, c), 0))
    in_specs = [row_spec, row_spec, row_spec,
                pl.BlockSpec((CHUNK, LANES), lambda dr, b, c: (blk(dr, b, c), 0)),
                pl.BlockSpec((LANES, CHUNK), lambda dr, b, c: (0, blk(dr, b, c)))]
    args = [q, k, v, gcols, grows]
    st_c = pl.BlockSpec((None, None, ML_HEADS, ML_DH, ML_DH), lambda dr, b, c: (b, dr, 0, 0, 0))
    st_n = pl.BlockSpec((None, None, ML_HEADS, 1, ML_DH), lambda dr, b, c: (b, dr, 0, 0, 0))
    st_m = pl.BlockSpec((None, None, ML_HEADS, 1, 1), lambda dr, b, c: (b, dr, 0, 0, 0))
    if init is not None:
        in_specs += [st_c, st_n, st_m]
        args += list(init)
    state_bytes = _nbytes((ML_HEADS, ML_DH, ML_DH), F32)
    return pl.pallas_call(
        functools.partial(_mlstm_scan_kernel, has_init=init is not None, nc=nc),
        grid=(2, nseq, nc), in_specs=in_specs,
        out_specs=[pl.BlockSpec((None, CHUNK, hd),
                                lambda dr, b, c: (dr, b * nc + c + dr * (nc - 1 - 2 * c), 0)),
                   st_c, st_n, st_m],
        out_shape=[jax.ShapeDtypeStruct((2, nseq * ln, hd), F32),
                   jax.ShapeDtypeStruct((nseq, 2, ML_HEADS, ML_DH, ML_DH), F32),
                   jax.ShapeDtypeStruct((nseq, 2, ML_HEADS, 1, ML_DH), F32),
                   jax.ShapeDtypeStruct((nseq, 2, ML_HEADS, 1, 1), F32)],
        name="mlstm_scan_l%d" % ln,
        compiler_params=pltpu.CompilerParams(
            dimension_semantics=("arbitrary",) * 3,
            vmem_limit_bytes=_vmem_limit(3 * _nbytes((CHUNK, hd), BF16), _nbytes((CHUNK, hd), F32),
                                         2 * state_bytes)),
    )(*args)


def _mlstm_post_fn(hf, hb, xc, z, norm_g, skip):
    h = hf + hb
    outs = []
    for hd in range(ML_HEADS):
        sl = slice(hd * ML_DH, (hd + 1) * ML_DH)
        hh = h[:, sl]
        mu = jnp.mean(hh, axis=-1, keepdims=True)
        var = jnp.mean(jnp.square(hh - mu), axis=-1, keepdims=True)
        outs.append((hh - mu) * lax.rsqrt(var + EPS))
    hn = jnp.concatenate(outs, axis=-1)
    y = hn * norm_g + skip * xc
    return (y * _silu(z),)


def _mlstm_mixer(u, geo, state, w_up, conv_w, w_q, w_k, w_v, w_gate, b_gate, norm_g, skip, w_down,
                 res_extras):
    inner = ML_HEADS * ML_DH
    t = u.shape[0]
    up = _linear(u, w_up, out_dtype=F32, name="ml_up")
    xc = _dwconv(up, conv_w, geo, n_ch=inner, post=_silu)
    q, k = _head_linear(xc, [w_q, w_k], [ML_DH ** -0.5, None])
    (v,) = _head_linear(up, [w_v], [None])
    wg = jnp.transpose(w_gate, (1, 0, 2)).reshape(3 * inner, 4 * ML_HEADS)
    wg = jnp.zeros((3 * inner, LANES), F32).at[:, :4 * ML_HEADS].set(wg)
    bg = jnp.zeros((1, LANES), F32).at[0, :4 * ML_HEADS].set(b_gate.reshape(-1))
    tm = _pick(t, (1024, 512, 256))
    (pre,) = _mm(
        [q, k, v], [wg, wg, wg], [(0, 0), (1, 1), (2, 2)], grid=(1, t // tm),
        a_specs=[pl.BlockSpec((tm, inner), lambda j, i: (i, 0))] * 3,
        b_specs=[pl.BlockSpec((inner, LANES), lambda j, i, r=r: (r, 0)) for r in range(3)],
        out_shapes=[jax.ShapeDtypeStruct((t, LANES), F32)],
        out_specs=[pl.BlockSpec((tm, LANES), lambda j, i: (i, 0))],
        epilogue=lambda accs, ex: (accs[0] + accs[1] + accs[2] + ex[0],),
        extras=[bg], extra_specs=[pl.BlockSpec((1, LANES), lambda j, i: (0, 0))], name="ml_gates")
    gcols = pl.pallas_call(
        _gate_prep_kernel, grid=(t // CHUNK,),
        in_specs=[pl.BlockSpec((CHUNK, LANES), lambda i: (i, 0))],
        out_specs=pl.BlockSpec((CHUNK, LANES), lambda i: (i, 0)),
        out_shape=jax.ShapeDtypeStruct((t, LANES), F32), name="mlstm_gate_prep",
        compiler_params=pltpu.CompilerParams(dimension_semantics=("arbitrary",)),
    )(pre)
    grows = gcols.T
    c0, n0, m0 = state
    hp, c_p, n_p, m_p = _mlstm_scan(q, k, v, gcols, grows, ln=geo.l_p, nseq=geo.n_p, row0=0)
    hs, _, _, _ = _mlstm_scan(q, k, v, gcols, grows, ln=geo.l_s, nseq=geo.n_s, row0=_tp(geo),
                              init=(c0, n0[:, :, :, None, :], m0[:, :, :, None, None]))
    ys = []
    for hdir, nrows, row0 in ((hp, _tp(geo), 0), (hs, t - _tp(geo), _tp(geo))):
        tmr = 256
        nb = nrows // tmr
        rb0 = row0 // tmr
        y = pl.pallas_call(
            functools.partial(_rowwise_kernel, nin=6, fn=_mlstm_post_fn),
            grid=(nb,),
            in_specs=[pl.BlockSpec((None, tmr, inner), lambda i: (0, i, 0)),
                      pl.BlockSpec((None, tmr, inner), lambda i: (1, i, 0)),
                      pl.BlockSpec((tmr, inner), lambda i: (rb0 + i, 0)),
                      pl.BlockSpec((tmr, inner), lambda i: (rb0 + i, 1)),
                      pl.BlockSpec((1, inner), lambda i: (0, 0)),
                      pl.BlockSpec((1, inner), lambda i: (0, 0))],
            out_specs=[pl.BlockSpec((tmr, inner), lambda i: (i, 0))],
            out_shape=[jax.ShapeDtypeStruct((nrows, inner), BF16)], name="mlstm_post",
            compiler_params=pltpu.CompilerParams(
                dimension_semantics=("arbitrary",),
                vmem_limit_bytes=_vmem_limit(5 * _nbytes((tmr, inner), F32))),
        )(hdir, hdir, xc, up, norm_g[None, :], skip[None, :])[0]
        ys.append(y)
    y = jnp.concatenate(ys, axis=0)
    x_new = _linear(y, w_down, out_dtype=F32, epilogue=_ep_residual, extras=res_extras,
                    extra_specs_fn=_residual_specs(geo), geo=geo, name="ml_down")
    return x_new, (c_p, n_p, m_p)


def _ln_silu_fn(y, g, b):
    mu = jnp.mean(y, axis=-1, keepdims=True)
    var = jnp.mean(jnp.square(y - mu), axis=-1, keepdims=True)
    return (_silu((y - mu) * lax.rsqrt(var + EPS) * g + b),)


def _conformer_mixer(u, geo, w1, b1, dw, b_dw, ln_g, ln_b, w2, b2, res_extras):
    d = D_MODEL
    b1r = b1[None, :]

    def glu(accs, ex):
        return ((accs[0] + ex[0]) * _sigmoid(accs[1] + ex[1]),)

    def especs(i_of, j_of, tm, tn):
        nj = d // tn
        return [pl.BlockSpec((1, tn), lambda *g: (0, j_of(*g))),
                pl.BlockSpec((1, tn), lambda *g: (0, nj + j_of(*g)))]

    y = _dual_linear(u, w1, w1, epilogue=glu, out_dtype=F32, n_out=d, w1_col=0, w2_col=1,
                     extras=[b1r, b1r], extra_specs_fn=especs, name="cv_glu")
    y = _dwconv(y, dw, geo, n_ch=d, bias=b_dw[None, :])
    (y,) = _rowwise(_ln_silu_fn, geo, rows=[(y, d, 0)], params=[(ln_g[None, :], d, 0), (ln_b[None, :], d, 0)],
                    outs=[(d, BF16)], tm=512, name="cv_ln_silu")
    return _linear(y, w2, out_dtype=F32, epilogue=_ep_residual_bias, extras=list(res_extras) + [b2[None, :]],
                   extra_specs_fn=_residual_bias_specs(geo), geo=geo, name="cv_out")


def _ep_swiglu(accs, ex):
    return (_silu(accs[0]) * accs[1],)


def _dense_ffn(u, geo, wg, wu, wd, layer, res_extras):
    _, d, f = wg.shape
    h = _dual_linear(u, wg.reshape(-1, f), wu.reshape(-1, f), epilogue=_ep_swiglu, out_dtype=BF16, n_out=f,
                     w_row=layer, name="ffn_up")
    return _linear(h, wd.reshape(-1, d), out_dtype=F32, epilogue=_ep_residual, extras=res_extras,
                   extra_specs_fn=_residual_specs(geo), geo=geo, k=f, w_row=layer, name="ffn_down")


def _lane_col(blk, idx):
    lane = lax.broadcasted_iota(jnp.int32, blk.shape, 1)
    return jnp.sum(jnp.where(lane == idx, blk, 0.0), axis=1, keepdims=True)


def _route_rank_kernel(info_ref, rank_ref, cnt_ref, carry_ref):
    @pl.when(pl.program_id(0) == 0)
    def _():
        carry_ref[...] = jnp.zeros_like(carry_ref)

    info = info_ref[...]
    lane = lax.broadcasted_iota(jnp.int32, info.shape, 1)
    lanef = lane.astype(F32)
    i1 = _lane_col(info, 0)
    i2 = _lane_col(info, 1)
    sel = jnp.where((lanef == i1) | (lanef == i2), 1.0, 0.0)
    r = lax.broadcasted_iota(jnp.int32, (CHUNK, CHUNK), 0)
    c = lax.broadcasted_iota(jnp.int32, (CHUNK, CHUNK), 1)
    strict = (c < r).astype(BF16)
    before = jnp.dot(strict, sel.astype(BF16), preferred_element_type=F32) + carry_ref[...]
    r1 = jnp.sum(jnp.where(lanef == i1, before, 0.0), axis=1, keepdims=True)
    r2 = jnp.sum(jnp.where(lanef == i2, before, 0.0), axis=1, keepdims=True)
    rank_ref[...] = jnp.where(lane == 0, r1, jnp.where(lane == 1, r2, 0.0))
    carry = carry_ref[...] + jnp.sum(sel, axis=0, keepdims=True)
    carry_ref[...] = carry
    cnt_ref[...] = carry


def _moe_up_kernel(te_ref, inv_ref, inv_next_ref, u_hbm, wg_ref, wu_ref, h_ref, buf, xs_b, sem, *, nt):
    i = pl.program_id(0)
    j = pl.program_id(1)
    n_used = te_ref[nt]
    _, tm, groups, _ = buf.shape

    def token_copy(src_tok, slot, dst_row):
        return pltpu.make_async_copy(u_hbm.at[src_tok], buf.at[slot, dst_row], sem.at[slot])

    def start_tile(idx_ref, slot):
        def body(r2, carry):
            token_copy(idx_ref[0, 2 * r2], slot, 2 * r2).start(priority=0)
            token_copy(idx_ref[0, 2 * r2 + 1], slot, 2 * r2 + 1).start(priority=1)
            return carry
        lax.fori_loop(0, tm // 2, body, 0, unroll=4)

    def finish_tile(slot):
        def body(r, carry):
            token_copy(0, slot, r).wait()
            return carry
        lax.fori_loop(0, tm, body, 0, unroll=8)
        xs_b[...] = jnp.concatenate([buf[slot, :, s, :] for s in range(groups)], axis=1).astype(BF16)

    @pl.when((j == 0) & (i < n_used))
    def _():
        @pl.when(i == 0)
        def _():
            start_tile(inv_ref, 0)

        for slot in range(2):
            @pl.when(i % 2 == slot)
            def _(slot=slot):
                @pl.when(i + 1 < n_used)
                def _():
                    start_tile(inv_next_ref, 1 - slot)

                finish_tile(slot)

    @pl.when(i < n_used)
    def _():
        a = xs_b[...]
        gate = jnp.dot(a, wg_ref[...].astype(BF16), preferred_element_type=F32)
        up = jnp.dot(a, wu_ref[...].astype(BF16), preferred_element_type=F32)
        h_ref[...] = (_silu(gate) * up).astype(h_ref.dtype)

    @pl.when(i >= n_used)
    def _():
        h_ref[...] = jnp.zeros_like(h_ref)


def _moe_down_kernel(te_ref, h_ref, w_ref, y_ref, w_b, *, nt):
    i = pl.program_id(1)

    @pl.when(i < te_ref[nt])
    def _():
        @pl.when((i == 0) | (te_ref[i] != te_ref[jnp.maximum(i - 1, 0)]))
        def _():
            w_b[...] = w_ref[...].astype(BF16)

        y_ref[...] = jnp.dot(h_ref[...], w_b[...], preferred_element_type=F32)

    @pl.when(i >= te_ref[nt])
    def _():
        y_ref[...] = jnp.zeros_like(y_ref)


def _moe_combine_kernel(p1_ref, p2_ref, ys_hbm, x_ref, g_ref, info_ref, o_ref, buf1, buf2, sem):
    tm = buf1.shape[0]

    def row_copy(src_row, buf, dst_row):
        return pltpu.make_async_copy(ys_hbm.at[pl.ds(src_row, 1), :], buf.at[pl.ds(dst_row, 1), :], sem)

    def issue(r, carry):
        row_copy(p1_ref[0, r], buf1, r).start(priority=0)
        row_copy(p2_ref[0, r], buf2, r).start(priority=1)
        return carry

    def drain(r, carry):
        row_copy(0, buf1, r).wait()
        row_copy(0, buf2, r).wait()
        return carry

    lax.fori_loop(0, tm, issue, 0, unroll=8)
    lax.fori_loop(0, tm, drain, 0, unroll=8)
    info = info_ref[...]
    y = _lane_col(info, 2) * buf1[...] + _lane_col(info, 3) * buf2[...]
    o_ref[...] = x_ref[...] + g_ref[...] * y


def _moe_ffn(u, info, geo, wg, wu, wd, layer, res_extras):
    t, groups, _ = u.shape
    d = groups * LANES
    _, ne, _, f = wg.shape
    tm = MOE_TM
    nt = (2 * t) // tm + ne
    rows = nt * tm
    rank, cnt = pl.pallas_call(
        _route_rank_kernel, grid=(t // CHUNK,),
        in_specs=[pl.BlockSpec((CHUNK, LANES), lambda i: (i, 0))],
        out_specs=[pl.BlockSpec((CHUNK, LANES), lambda i: (i, 0)), pl.BlockSpec((1, LANES), lambda i: (0, 0))],
        out_shape=[jax.ShapeDtypeStruct((t, LANES), F32), jax.ShapeDtypeStruct((1, LANES), F32)],
        scratch_shapes=[pltpu.VMEM((1, LANES), F32)], name="moe_rank",
        compiler_params=pltpu.CompilerParams(dimension_semantics=("arbitrary",)),
    )(info)
    counts = cnt[0, :ne].astype(jnp.int32)
    gsz = (counts + tm - 1) // tm * tm
    ends = jnp.cumsum(gsz)
    off = ends - gsz
    e1 = info[:, 0].astype(jnp.int32)
    e2 = info[:, 1].astype(jnp.int32)
    pos1 = off[e1] + rank[:, 0].astype(jnp.int32)
    pos2 = off[e2] + rank[:, 1].astype(jnp.int32)
    tok = jnp.arange(t, dtype=jnp.int32)
    inv = jnp.zeros((rows,), jnp.int32).at[jnp.concatenate([pos1, pos2])].set(jnp.concatenate([tok, tok]))
    tile_expert = jnp.sum(jnp.arange(nt, dtype=jnp.int32)[:, None] * tm >= ends[None, :], axis=1)
    te = jnp.concatenate([jnp.minimum(tile_expert, ne - 1), (ends[-1:] // tm)]).astype(jnp.int32)

    smem_row = lambda n: pl.BlockSpec((None, 1, n), lambda i, *_: (i, 0, 0), memory_space=pltpu.SMEM)

    def used(i, te_ref):
        return jnp.minimum(i, te_ref[nt] - 1)

    tn = _pick(f, (512, 256, 128))
    inv3 = inv.reshape(nt, 1, tm)
    h = pl.pallas_call(
        functools.partial(_moe_up_kernel, nt=nt),
        grid_spec=pltpu.PrefetchScalarGridSpec(
            num_scalar_prefetch=1, grid=(nt, f // tn),
            in_specs=[pl.BlockSpec((None, 1, tm), lambda i, j, te_ref: (i, 0, 0), memory_space=pltpu.SMEM),
                      pl.BlockSpec((None, 1, tm), lambda i, j, te_ref: (jnp.minimum(i + 1, nt - 1), 0, 0),
                                   memory_space=pltpu.SMEM),
                      pl.BlockSpec(memory_space=pl.ANY),
                      pl.BlockSpec((None, None, d, tn), lambda i, j, te_ref: (layer, te_ref[i], 0, j)),
                      pl.BlockSpec((None, None, d, tn), lambda i, j, te_ref: (layer, te_ref[i], 0, j))],
            out_specs=pl.BlockSpec((tm, tn), lambda i, j, te_ref: (i, j)),
            scratch_shapes=[pltpu.VMEM((2, tm, groups, LANES), F32), pltpu.VMEM((tm, d), BF16),
                            pltpu.SemaphoreType.DMA((2,))]),
        out_shape=jax.ShapeDtypeStruct((rows, f), BF16), name="moe_up",
        compiler_params=pltpu.CompilerParams(
            dimension_semantics=("arbitrary",) * 2,
            vmem_limit_bytes=_vmem_limit(_nbytes((tm, d), F32), _nbytes((d, tn), F32) * 3, _nbytes((tm, d), BF16),
                                         _nbytes((tm, tn), F32) * 3)),
    )(te, inv3, inv3, u, wg, wu)
    tno = 512
    ys = pl.pallas_call(
        functools.partial(_moe_down_kernel, nt=nt),
        grid_spec=pltpu.PrefetchScalarGridSpec(
            num_scalar_prefetch=1, grid=(d // tno, nt),
            in_specs=[pl.BlockSpec((tm, f), lambda j, i, te_ref: (used(i, te_ref), 0)),
                      pl.BlockSpec((None, None, f, tno), lambda j, i, te_ref: (layer, te_ref[i], 0, j))],
            out_specs=pl.BlockSpec((tm, tno), lambda j, i, te_ref: (i, j)),
            scratch_shapes=[pltpu.VMEM((f, tno), BF16)]),
        out_shape=jax.ShapeDtypeStruct((rows, d), F32), name="moe_down",
        compiler_params=pltpu.CompilerParams(
            dimension_semantics=("arbitrary",) * 2,
            vmem_limit_bytes=_vmem_limit(_nbytes((f, tno), F32), _nbytes((f, tno), BF16),
                                         _nbytes((tm, f), BF16), _nbytes((tm, tno), F32))),
    )(te, h, wd)

    x, gate = res_extras
    tc = CHUNK
    cond = _cond_of_block(geo, tc)
    return pl.pallas_call(
        _moe_combine_kernel, grid=(t // tc,),
        in_specs=[smem_row(tc), smem_row(tc), pl.BlockSpec(memory_space=pl.ANY),
                  pl.BlockSpec((tc, d), lambda i: (i, 0)),
                  pl.BlockSpec((None, 1, d), lambda i: (cond(i), 0, 0)),
                  pl.BlockSpec((tc, LANES), lambda i: (i, 0))],
        out_specs=pl.BlockSpec((tc, d), lambda i: (i, 0)),
        out_shape=jax.ShapeDtypeStruct((t, d), F32),
        scratch_shapes=[pltpu.VMEM((tc, d), F32), pltpu.VMEM((tc, d), F32), pltpu.SemaphoreType.DMA(())],
        name="moe_combine",
        compiler_params=pltpu.CompilerParams(dimension_semantics=("arbitrary",)),
    )(pos1.reshape(t // tc, 1, tc), pos2.reshape(t // tc, 1, tc), ys, x, gate, info)


def _trunk(x, conds, geo, state, p):
    depth = p["w_ada"].shape[0]
    d = D_MODEL
    nco = conds.shape[0]
    cpad = jnp.zeros((8, d), F32).at[:nco].set(conds)
    w_ada = p["w_ada"]
    tn = 512
    nj = 6 * d // tn
    (mod,) = _mm(
        [cpad], [w_ada], [(0, 0)], grid=(depth, nj, 1),
        a_specs=[pl.BlockSpec((8, d), lambda l, j, i: (0, 0))],
        b_specs=[pl.BlockSpec((None, d, tn), lambda l, j, i: (l, 0, j))],
        out_shapes=[jax.ShapeDtypeStruct((depth, 8, 6 * d), F32)],
        out_specs=[pl.BlockSpec((None, 8, tn), lambda l, j, i: (l, 0, j))],
        epilogue=lambda accs, ex: (accs[0] + ex[0],), a_fn=_silu,
        extras=[p["b_ada"][:, None, :]],
        extra_specs=[pl.BlockSpec((None, 1, tn), lambda l, j, i: (l, 0, j))], name="ada_mod")
    mod = mod[:, :nco].reshape(depth, nco, 6, 1, d)

    new_states = []
    for i in range(depth):
        sh1, sc1, g1, sh2, sc2, g2 = (mod[i, :, r] for r in range(6))
        u = _norm_mod(x, p["norm_mix_g"][i][None, :], sc1, sh1, geo)
        kind, j = i % 3, i // 3
        res = [x, g1]
        if kind == 0:
            x = _hyena_mixer(u, geo, p["hy_w_in"][j], p["hy_short"][j], p["hy_filt_w1"][j], p["hy_filt_b1"][j],
                             p["hy_filt_w2"][j], p["hy_filt_b2"][j], p["hy_filt_w3"][j], p["hy_filt_bias"][j],
                             p["hy_w_out"][j], res)
        elif kind == 1:
            st = tuple(s[:, j] for s in state)
            x, new_st = _mlstm_mixer(u, geo, st, p["ml_w_up"][j], p["ml_conv"][j], p["ml_w_q"][j],
                                     p["ml_w_k"][j], p["ml_w_v"][j], p["ml_w_gate"][j], p["ml_b_gate"][j],
                                     p["ml_norm_g"][j], p["ml_skip"][j], p["ml_w_down"][j], res)
            new_states.append(new_st)
        else:
            x = _conformer_mixer(u, geo, p["cv_w1"][j], p["cv_b1"][j], p["cv_dw"][j], p["cv_b_dw"][j],
                                 p["cv_ln_g"][j], p["cv_ln_b"][j], p["cv_w2"][j], p["cv_b2"][j], res)
        jf = i // 2
        g_ffn = p["norm_ffn_g"][i][None, :]
        if i % 2 == 0:
            u = _norm_mod(x, g_ffn, sc2, sh2, geo)
            x = _dense_ffn(u, geo, p["ff_w_gate"], p["ff_w_up"], p["ff_w_down"], jf, [x, g2])
        else:
            u, info = _norm_mod_route(x, g_ffn, sc2, sh2, p["moe_router"][jf], geo)
            x = _moe_ffn(u, info, geo, p["moe_w_gate"], p["moe_w_up"], p["moe_w_down"], jf, [x, g2])
    fg = p["final_norm_g"][None, :]
    outs = []
    for nrows, row0 in ((_tp(geo), 0), (_tt(geo) - _tp(geo), _tp(geo))):
        tm = 512
        (y,) = _rowwise(lambda xb, g: (_rms(xb, g),), geo, rows=[(x, d, 0)], params=[(fg, d, 0)],
                        outs=[(d, F32)], tm=tm, n_rows=nrows, row_block0=row0 // tm, name="final_norm")
        outs.append(y)
    new_state = tuple(jnp.stack(parts, axis=1) for parts in zip(*new_states))
    return outs[0], outs[1], new_state


def kernel(x_prompt, x_sample, state_mlstm_C, state_mlstm_n, state_mlstm_m, c, c_ctx, w_ada, b_ada, norm_mix_g, norm_ffn_g, hy_w_in, hy_short, hy_filt_w1, hy_filt_b1, hy_filt_w2, hy_filt_b2, hy_filt_w3, hy_filt_bias, hy_w_out, ml_w_up, ml_conv, ml_w_q, ml_w_k, ml_w_v, ml_w_gate, ml_b_gate, ml_norm_g, ml_skip, ml_w_down, cv_w1, cv_b1, cv_dw, cv_b_dw, cv_ln_g, cv_ln_b, cv_w2, cv_b2, ff_w_gate, ff_w_up, ff_w_down, moe_router, moe_w_gate, moe_w_up, moe_w_down, final_norm_g):
    p = dict(w_ada=w_ada, b_ada=b_ada, norm_mix_g=norm_mix_g, norm_ffn_g=norm_ffn_g, hy_w_in=hy_w_in,
             hy_short=hy_short, hy_filt_w1=hy_filt_w1, hy_filt_b1=hy_filt_b1, hy_filt_w2=hy_filt_w2,
             hy_filt_b2=hy_filt_b2, hy_filt_w3=hy_filt_w3, hy_filt_bias=hy_filt_bias, hy_w_out=hy_w_out,
             ml_w_up=ml_w_up, ml_conv=ml_conv, ml_w_q=ml_w_q, ml_w_k=ml_w_k, ml_w_v=ml_w_v,
             ml_w_gate=ml_w_gate, ml_b_gate=ml_b_gate, ml_norm_g=ml_norm_g, ml_skip=ml_skip,
             ml_w_down=ml_w_down, cv_w1=cv_w1, cv_b1=cv_b1, cv_dw=cv_dw, cv_b_dw=cv_b_dw, cv_ln_g=cv_ln_g,
             cv_ln_b=cv_ln_b, cv_w2=cv_w2, cv_b2=cv_b2, ff_w_gate=ff_w_gate, ff_w_up=ff_w_up,
             ff_w_down=ff_w_down, moe_router=moe_router, moe_w_gate=moe_w_gate, moe_w_up=moe_w_up,
             moe_w_down=moe_w_down, final_norm_g=final_norm_g)
    n_p, l_p, d = x_prompt.shape
    n_s, l_s, _ = x_sample.shape
    geo = Geo(n_p, l_p, n_s, l_s)
    x = jnp.concatenate([x_prompt.reshape(n_p * l_p, d), x_sample.reshape(n_s * l_s, d)], axis=0)
    conds = jnp.concatenate([c_ctx[None, :], c], axis=0)
    y_p, y_s, (c_new, n_new, m_new) = _trunk(
        x, conds, geo, (state_mlstm_C, state_mlstm_n, state_mlstm_m), p)
    n_ml = state_mlstm_C.shape[1]
    return (y_p.reshape(n_p, l_p, d), y_s.reshape(n_s, l_s, d),
            c_new.reshape(n_p, n_ml, 2, ML_HEADS, ML_DH, ML_DH),
            n_new.reshape(n_p, n_ml, 2, ML_HEADS, ML_DH),
            m_new.reshape(n_p, n_ml, 2, ML_HEADS))
```

```python
import collections
import functools
import math

import jax
import jax.numpy as jnp
from jax import lax
from jax.experimental import pallas as pl
from jax.experimental.pallas import tpu as pltpu

F32 = jnp.float32
BF16 = jnp.bfloat16
HIGHEST = lax.Precision.HIGHEST

D_MODEL = 1024
HY_BANDS = 16
HY_SLOW_DECAY = -math.log(1e-2) / 1.5
HY_FAST_DECAY = -math.log(1e-2) / 0.3
HY_SHIFT = 0.05
ML_HEADS = 8
ML_DH = 256
N_EXPERTS = 8
EPS = 1e-6

LANES = 128
SUBLANES = 8
CHUNK = 256
CONV_HALO = 16
CONV_ROWS = 64
MOE_TM = 1024
VMEM_CAP = 56 * 1024 * 1024
COL_TILE_BUDGET = 40 * 1024 * 1024

Geo = collections.namedtuple("Geo", "n_p l_p n_s l_s")


def _tp(geo):
    return geo.n_p * geo.l_p


def _tt(geo):
    return geo.n_p * geo.l_p + geo.n_s * geo.l_s


def _cond_of_block(geo, rows):
    tp = _tp(geo)

    def f(i):
        r0 = i * rows
        return jnp.where(r0 < tp, 0, 1 + (r0 - tp) // geo.l_s)

    return f


def _vmem_limit(*nbytes):
    est = 2 * sum(nbytes) + (8 << 20)
    return int(min(max(est, 32 << 20), VMEM_CAP))


def _nbytes(shape, dtype):
    n = 1
    for s in shape:
        if s is not None:
            n *= s
    return n * jnp.dtype(dtype).itemsize


def _silu(x):
    return x * (1.0 / (1.0 + jnp.exp(-x)))


def _sigmoid(x):
    return 1.0 / (1.0 + jnp.exp(-x))


def _mm_kernel(*refs, na, nb, ne, no, pairs, cast_b, epilogue, a_fn, inner_axis):
    a_refs = refs[:na]
    b_refs = refs[na:na + nb]
    e_refs = refs[na + nb:na + nb + ne]
    o_refs = refs[na + nb + ne:na + nb + ne + no]
    scratch = refs[na + nb + ne + no:]
    first = pl.program_id(inner_axis) == 0
    b_src = []
    si = 0
    for j in range(nb):
        if cast_b[j]:
            s_ref = scratch[si]
            si += 1

            @pl.when(first)
            def _(s_ref=s_ref, b_ref=b_refs[j]):
                s_ref[...] = b_ref[...].astype(BF16)

            b_src.append(s_ref)
        else:
            b_src.append(b_refs[j])
    a_vals = {}
    accs = []
    for (i, j) in pairs:
        if i not in a_vals:
            a = a_refs[i][...]
            if a_fn is not None:
                a = a_fn(a)
            a_vals[i] = a.astype(BF16)
        accs.append(jnp.dot(a_vals[i], b_src[j][...], preferred_element_type=F32))
    outs = epilogue(accs, [e[...] for e in e_refs])
    for o_ref, val in zip(o_refs, outs):
        o_ref[...] = val.astype(o_ref.dtype)


def _mm(a_list, b_list, pairs, *, grid, a_specs, b_specs, out_shapes, out_specs,
        epilogue, extras=(), extra_specs=(), a_fn=None, name="mm"):
    cast_b = tuple(b.dtype != BF16 for b in b_list)
    scratch = [pltpu.VMEM(tuple(s for s in spec.block_shape if s is not None), BF16)
               for b, spec, c in zip(b_list, b_specs, cast_b) if c]
    kern = functools.partial(
        _mm_kernel, na=len(a_list), nb=len(b_list), ne=len(extras), no=len(out_shapes),
        pairs=tuple(pairs), cast_b=cast_b, epilogue=epilogue, a_fn=a_fn, inner_axis=len(grid) - 1)
    sizes = [_nbytes(s.block_shape, a.dtype) for a, s in zip(a_list, a_specs)]
    sizes += [_nbytes(s.block_shape, b.dtype) for b, s in zip(b_list, b_specs)]
    sizes += [_nbytes(s.block_shape, e.dtype) for e, s in zip(extras, extra_specs)]
    sizes += [_nbytes(s.block_shape, o.dtype) for o, s in zip(out_shapes, out_specs)]
    sizes += [_nbytes(s.block_shape, BF16) // 2 for s, c in zip(b_specs, cast_b) if c]
    res = pl.pallas_call(
        kern, grid=grid,
        in_specs=list(a_specs) + list(b_specs) + list(extra_specs),
        out_specs=list(out_specs), out_shape=list(out_shapes),
        scratch_shapes=scratch, name=name,
        compiler_params=pltpu.CompilerParams(
            dimension_semantics=("arbitrary",) * len(grid), vmem_limit_bytes=_vmem_limit(*sizes)),
    )(*a_list, *b_list, *extras)
    return res


def _pick(n, prefs):
    for p in prefs:
        if n % p == 0:
            return p
    return n


def _row_tile(m, geo):
    base = math.gcd(m, geo.l_s, _tp(geo) or m) if geo is not None else m
    return _pick(base, (1024, 512, 256, 128, 8))


def _col_tile(k, n_out, n_w, tm, out_bytes):
    for tn in (1408, 1024, 512, 256):
        need = n_w * k * tn * (2 * 4 + 2) + 2 * tm * k * 2 + 2 * tm * tn * (out_bytes + 4)
        if n_out % tn == 0 and need <= COL_TILE_BUDGET:
            return tn
    return LANES


def _linear(a, w, *, out_dtype, epilogue=None, extras=(), extra_specs_fn=None, tm=None, tn=None,
            a_col=0, w_col=0, w_row=0, n_out=None, a_fn=None, k=None, geo=None, name="linear"):
    m = a.shape[0]
    k = k or w.shape[0]
    n_out = n_out or w.shape[1]
    tm = tm or _row_tile(m, geo)
    tn = tn or _col_tile(k, n_out, 1, tm, jnp.dtype(out_dtype).itemsize)
    grid = (n_out // tn, m // tm)
    jo = w_col * (n_out // tn)
    especs = extra_specs_fn(lambda j, i: i, lambda j, i: j, tm, tn) if extras else ()
    ep = epilogue or (lambda accs, ex: (accs[0],))
    (out,) = _mm(
        [a], [w], [(0, 0)], grid=grid,
        a_specs=[pl.BlockSpec((tm, k), lambda j, i: (i, a_col))],
        b_specs=[pl.BlockSpec((k, tn), lambda j, i: (w_row, jo + j))],
        out_shapes=[jax.ShapeDtypeStruct((m, n_out), out_dtype)],
        out_specs=[pl.BlockSpec((tm, tn), lambda j, i: (i, j))],
        epilogue=ep, extras=extras, extra_specs=especs, a_fn=a_fn, name=name)
    return out


def _residual_specs(geo):
    def fn(i_of, j_of, tm, tn):
        cond = _cond_of_block(geo, tm)
        return [pl.BlockSpec((tm, tn), lambda *g: (i_of(*g), j_of(*g))),
                pl.BlockSpec((None, 1, tn), lambda *g: (cond(i_of(*g)), 0, j_of(*g)))]
    return fn


def _residual_bias_specs(geo):
    base = _residual_specs(geo)

    def fn(i_of, j_of, tm, tn):
        return base(i_of, j_of, tm, tn) + [pl.BlockSpec((1, tn), lambda *g: (0, j_of(*g)))]
    return fn


def _ep_residual(accs, ex):
    return (ex[0] + ex[1] * accs[0],)


def _ep_residual_bias(accs, ex):
    return (ex[0] + ex[1] * (accs[0] + ex[2]),)


def _dual_linear(a, w1, w2, *, epilogue, out_dtype, n_out, w1_col=0, w2_col=0, extras=(),
                 extra_specs_fn=None, tm=None, tn=None, geo=None, w_row=0, name="dual_linear"):
    m, k = a.shape
    tm = tm or _row_tile(m, geo)
    tn = tn or _col_tile(k, n_out, 2, tm, jnp.dtype(out_dtype).itemsize)
    nj = n_out // tn
    grid = (nj, m // tm)
    especs = extra_specs_fn(lambda j, i: i, lambda j, i: j, tm, tn) if extras else ()
    (out,) = _mm(
        [a], [w1, w2], [(0, 0), (0, 1)], grid=grid,
        a_specs=[pl.BlockSpec((tm, k), lambda j, i: (i, 0))],
        b_specs=[pl.BlockSpec((k, tn), lambda j, i: (w_row, w1_col * nj + j)),
                 pl.BlockSpec((k, tn), lambda j, i: (w_row, w2_col * nj + j))],
        out_shapes=[jax.ShapeDtypeStruct((m, n_out), out_dtype)],
        out_specs=[pl.BlockSpec((tm, tn), lambda j, i: (i, j))],
        epilogue=epilogue, extras=extras, extra_specs=especs, name=name)
    return out


def _rowwise_kernel(*refs, nin, fn):
    ins = [r[...] for r in refs[:nin]]
    outs = fn(*ins)
    for o_ref, val in zip(refs[nin:], outs):
        o_ref[...] = val.astype(o_ref.dtype)


def _rowwise(fn, geo, *, rows=(), params=(), conds=(), outs, tm, n_rows=None, row_block0=0, name="rowwise"):
    n_rows = n_rows or rows[0][0].shape[0]
    cond = _cond_of_block(geo, tm)
    in_specs, args, sizes = [], [], []
    for arr, w, cb in rows:
        in_specs.append(pl.BlockSpec((tm, w), lambda i, cb=cb: (i + row_block0, cb)))
        args.append(arr)
        sizes.append(_nbytes((tm, w), arr.dtype))
    for arr, w, cb in params:
        in_specs.append(pl.BlockSpec((1, w), lambda i, cb=cb: (0, cb)))
        args.append(arr)
    for arr in conds:
        w = arr.shape[-1]
        in_specs.append(pl.BlockSpec((None, 1, w), lambda i: (cond(i + row_block0), 0, 0)))
        args.append(arr)
    out_shapes = [jax.ShapeDtypeStruct((n_rows, w), dt) for w, dt in outs]
    out_specs = [pl.BlockSpec((tm, w), lambda i: (i, 0)) for w, dt in outs]
    sizes += [_nbytes((tm, w), dt) for w, dt in outs]
    return pl.pallas_call(
        functools.partial(_rowwise_kernel, nin=len(args), fn=fn),
        grid=(n_rows // tm,), in_specs=in_specs, out_specs=out_specs, out_shape=out_shapes, name=name,
        compiler_params=pltpu.CompilerParams(
            dimension_semantics=("arbitrary",), vmem_limit_bytes=_vmem_limit(*sizes, *sizes)),
    )(*args)


def _rms(x, g):
    return x * lax.rsqrt(jnp.mean(x * x, axis=-1, keepdims=True) + EPS) * g


def _norm_mod_fn(x, g, sc, sh):
    return (_rms(x, g) * (1.0 + sc) + sh,)


def _norm_mod(x, g, sc, sh, geo):
    (u,) = _rowwise(_norm_mod_fn, geo, rows=[(x, D_MODEL, 0)], params=[(g, D_MODEL, 0)],
                    conds=[sc, sh], outs=[(D_MODEL, BF16)], tm=512, name="norm_mod")
    return u


def _norm_mod_route_fn(x, g, router, sc, sh):
    u = _rms(x, g) * (1.0 + sc) + sh
    logits = jnp.dot(u, router, preferred_element_type=F32, precision=HIGHEST)
    lane = lax.broadcasted_iota(jnp.int32, logits.shape, 1)
    neg = jnp.float32(-jnp.inf)
    logits = jnp.where(lane < N_EXPERTS, logits, neg)
    v1 = jnp.max(logits, axis=-1, keepdims=True)
    i1 = jnp.min(jnp.where(logits == v1, lane, LANES), axis=-1, keepdims=True)
    rest = jnp.where(lane == i1, neg, logits)
    v2 = jnp.max(rest, axis=-1, keepdims=True)
    i2 = jnp.min(jnp.where(rest == v2, lane, LANES), axis=-1, keepdims=True)
    e2 = jnp.exp(v2 - v1)
    p1 = 1.0 / (1.0 + e2)
    p2 = e2 / (1.0 + e2)
    info = jnp.where(lane == 0, i1.astype(F32),
                     jnp.where(lane == 1, i2.astype(F32),
                               jnp.where(lane == 2, p1, jnp.where(lane == 3, p2, 0.0))))
    return u, info


def _norm_mod_route_kernel(x_ref, g_ref, r_ref, sc_ref, sh_ref, u_ref, info_ref):
    u, info = _norm_mod_route_fn(x_ref[...], g_ref[...], r_ref[...], sc_ref[...], sh_ref[...])
    for s in range(u_ref.shape[1]):
        u_ref[:, s, :] = u[:, s * LANES:(s + 1) * LANES]
    info_ref[...] = info


def _norm_mod_route(x, g, sc, sh, router, geo):
    tm = 256
    cond = _cond_of_block(geo, tm)
    n = x.shape[0]
    router_p = jnp.zeros((D_MODEL, LANES), F32).at[:, :N_EXPERTS].set(router)
    row = pl.BlockSpec((tm, D_MODEL), lambda i: (i, 0))
    cspec = pl.BlockSpec((None, 1, D_MODEL), lambda i: (cond(i), 0, 0))
    groups = D_MODEL // LANES
    return pl.pallas_call(
        _norm_mod_route_kernel,
        grid=(n // tm,),
        in_specs=[row, pl.BlockSpec((1, D_MODEL), lambda i: (0, 0)),
                  pl.BlockSpec((D_MODEL, LANES), lambda i: (0, 0)), cspec, cspec],
        out_specs=[pl.BlockSpec((tm, groups, LANES), lambda i: (i, 0, 0)),
                   pl.BlockSpec((tm, LANES), lambda i: (i, 0))],
        out_shape=[jax.ShapeDtypeStruct((n, groups, LANES), F32), jax.ShapeDtypeStruct((n, LANES), F32)],
        name="norm_mod_route", compiler_params=pltpu.CompilerParams(dimension_semantics=("arbitrary",)),
    )(x, g, router_p, sc, sh)


def _dwconv_kernel(x_ref, w_ref, *rest, taps, geo, rb, post, has_bias):
    if has_bias:
        b_ref, o_ref, pad_ref = rest
    else:
        o_ref, pad_ref = rest
    half = (taps - 1) // 2
    tc = x_ref.shape[1]
    n_prompt_blocks = _tp(geo) // rb
    first = CONV_HALO - half
    win0 = first // SUBLANES * SUBLANES
    shift0 = first - win0
    win_rows = -(-(shift0 + taps - 1 + CONV_ROWS) // SUBLANES) * SUBLANES

    def run(nseq, ln):
        for s in range(nseq):
            base = s * ln
            zeros = jnp.zeros((CONV_HALO, tc), F32)
            pad_ref[0:CONV_HALO, :] = zeros
            pad_ref[CONV_HALO + ln:2 * CONV_HALO + ln, :] = zeros
            pad_ref[CONV_HALO:CONV_HALO + ln, :] = x_ref[base:base + ln, :]
            for r0 in range(0, ln, CONV_ROWS):
                acc = jnp.zeros((CONV_ROWS, tc), F32)
                win = pad_ref[win0 + r0:win0 + r0 + win_rows, :]
                for r in range(SUBLANES):
                    group = [kk for kk in range(taps) if (shift0 + kk) % SUBLANES == r]
                    if not group:
                        continue
                    rot = win if r == 0 else pltpu.roll(win, win_rows - r, 0)
                    for kk in group:
                        m0 = (shift0 + kk) // SUBLANES * SUBLANES
                        acc = acc + w_ref[kk:kk + 1, :] * rot[m0:m0 + CONV_ROWS]
                if has_bias:
                    acc = acc + b_ref[...]
                o_ref[base + r0:base + r0 + CONV_ROWS, :] = post(acc)

    i = pl.program_id(0)
    if n_prompt_blocks > 0:
        @pl.when(i < n_prompt_blocks)
        def _():
            run(rb // geo.l_p, geo.l_p)

    @pl.when(i >= n_prompt_blocks)
    def _():
        run(1, geo.l_s)


def _dwconv(x, w, geo, *, n_ch, bias=None, post=None, tc=256):
    taps = w.shape[0]
    rb = geo.l_s
    t = x.shape[0]
    post = post or (lambda v: v)
    in_specs = [pl.BlockSpec((rb, tc), lambda i, j: (i, j)), pl.BlockSpec((taps, tc), lambda i, j: (0, j))]
    args = [x, w]
    if bias is not None:
        in_specs.append(pl.BlockSpec((1, tc), lambda i, j: (0, j)))
        args.append(bias)
    blk = _nbytes((rb, tc), F32)
    return pl.pallas_call(
        functools.partial(_dwconv_kernel, taps=taps, geo=geo, rb=rb, post=post, has_bias=bias is not None),
        grid=(t // rb, n_ch // tc), in_specs=in_specs,
        out_specs=pl.BlockSpec((rb, tc), lambda i, j: (i, j)),
        out_shape=jax.ShapeDtypeStruct((t, n_ch), F32),
        scratch_shapes=[pltpu.VMEM((rb + 2 * CONV_HALO, tc), F32)], name="dwconv%d" % taps,
        compiler_params=pltpu.CompilerParams(
            dimension_semantics=("arbitrary", "arbitrary"), vmem_limit_bytes=_vmem_limit(blk, blk, blk)),
    )(*args)


def _dft_tables(ln):
    kb = 64
    c3 = jnp.arange(ln, dtype=jnp.int32)[None, None, :]
    a3 = jnp.arange(ln // kb, dtype=jnp.int32)[:, None, None]
    b3 = jnp.arange(kb, dtype=jnp.int32)[None, :, None]
    ang_a = ((kb * a3 * c3) % (2 * ln)).astype(F32) * (math.pi / ln)
    ang_b = ((b3 * c3) % (2 * ln)).astype(F32) * (math.pi / ln)
    ca, sa, cb, sb = jnp.cos(ang_a), jnp.sin(ang_a), jnp.cos(ang_b), jnp.sin(ang_b)
    cos = ca * cb - sa * sb
    sin = sa * cb + ca * sb
    row0 = (a3 == 0) & (b3 == 0)
    col0 = c3 == 0
    alt_col = jnp.where(c3 % 2 == 0, 1.0, -1.0).astype(F32)
    alt_row = jnp.where(b3 % 2 == 0, 1.0, -1.0).astype(F32)
    half = jnp.arange(2, dtype=jnp.int32)[:, None, None, None]
    fwd = jnp.where(half == 0, cos[None], jnp.where(row0, alt_col, -sin)[None])
    fwd = fwd.reshape(2 * ln, ln).astype(BF16)
    scale = 1.0 / (2 * ln)
    inv_c = (jnp.where(col0, 1.0, 2.0 * cos) * scale).reshape(ln, ln).astype(BF16)
    inv_s = (jnp.where(col0, alt_row, -2.0 * sin) * scale).reshape(ln, ln).astype(BF16)
    return fwd, inv_c, inv_s


def _filter_hidden_kernel(w1_ref, b1_ref, w2_ref, b2_ref, o_ref, *, ln):
    pos = lax.broadcasted_iota(jnp.int32, (ln, LANES), 0).astype(F32) / ln
    lane = lax.broadcasted_iota(jnp.int32, (ln, LANES), 1)
    band = jnp.where(lane <= HY_BANDS, lane, lane - HY_BANDS).astype(F32)
    ang = (2.0 * math.pi) * pos * band
    feats = jnp.where(lane == 0, pos,
                      jnp.where(lane <= HY_BANDS, jnp.cos(ang),
                                jnp.where(lane <= 2 * HY_BANDS, jnp.sin(ang), 0.0)))
    h = jnp.sin(jnp.dot(feats, w1_ref[...], preferred_element_type=F32, precision=HIGHEST) + b1_ref[...])
    o_ref[...] = jnp.sin(jnp.dot(h, w2_ref[...], preferred_element_type=F32, precision=HIGHEST) + b2_ref[...])


def _filter_kernel(h_ref, w3_ref, hw_ref, ss_ref, *, ln):
    tn = hw_ref.shape[1]
    j = pl.program_id(0)
    h = jnp.dot(h_ref[...], w3_ref[...], preferred_element_type=F32, precision=HIGHEST)
    ch = (j * tn) % D_MODEL + lax.broadcasted_iota(jnp.int32, (1, tn), 1)
    step = (HY_FAST_DECAY - HY_SLOW_DECAY) / (D_MODEL - 1)
    deltas = HY_SLOW_DECAY + ch.astype(F32) * step
    tcol = lax.broadcasted_iota(jnp.int32, (ln, 1), 0).astype(F32) / ln
    h = h * (jnp.exp(-tcol * deltas) + HY_SHIFT)
    hw_ref[...] = h
    ss_ref[...] = jnp.sum(h * h, axis=0, keepdims=True)


def _hyena_filters(ln, w1, b1, w2, b2, w3, fwd):
    hid = w1.shape[1]
    ncol = w3.shape[1]
    tn = 512
    w1p = jnp.zeros((LANES, hid), F32).at[:w1.shape[0]].set(w1)
    full = lambda shape: pl.BlockSpec(shape, lambda j: (0, 0))
    hidden = pl.pallas_call(
        functools.partial(_filter_hidden_kernel, ln=ln),
        grid=(1,),
        in_specs=[full((LANES, hid)), full((1, hid)), full((hid, hid)), full((1, hid))],
        out_specs=full((ln, hid)),
        out_shape=jax.ShapeDtypeStruct((ln, hid), F32),
        name="hy_filter_hidden",
        compiler_params=pltpu.CompilerParams(dimension_semantics=("arbitrary",)),
    )(w1p, b1[None, :], w2, b2[None, :])
    hw, ss = pl.pallas_call(
        functools.partial(_filter_kernel, ln=ln),
        grid=(ncol // tn,),
        in_specs=[full((ln, hid)), pl.BlockSpec((hid, tn), lambda j: (0, j))],
        out_specs=[pl.BlockSpec((ln, tn), lambda j: (0, j)), pl.BlockSpec((1, tn), lambda j: (0, j))],
        out_shape=[jax.ShapeDtypeStruct((ln, ncol), F32), jax.ShapeDtypeStruct((1, ncol), F32)],
        name="hy_filter_cols",
        compiler_params=pltpu.CompilerParams(dimension_semantics=("arbitrary",),
                                             vmem_limit_bytes=_vmem_limit(_nbytes((ln, tn), F32) * 4)),
    )(hidden, w3)
    spec = _linear(fwd, hw, out_dtype=F32, tm=_pick(2 * ln, (1024, 512)), name="hy_filter_dft")
    half = ncol // 2
    tk = _pick(ln, (512, 256))

    def assemble(fc, fs, bc, bs, ssf, ssb):
        scale = lax.rsqrt(ssf + ssb + EPS)
        row0 = (pl.program_id(0) == 0) & (lax.broadcasted_iota(jnp.int32, fc.shape, 0) == 0)
        gc = (fc + bc) * scale
        gs = jnp.where(row0, fs + bs, fs - bs) * scale
        return gc, gs

    nk = ln // tk
    nj = half // tn
    return pl.pallas_call(
        functools.partial(_rowwise_kernel, nin=6, fn=assemble),
        grid=(nk, nj),
        in_specs=[pl.BlockSpec((tk, tn), lambda i, j: (i, j)),
                  pl.BlockSpec((tk, tn), lambda i, j: (nk + i, j)),
                  pl.BlockSpec((tk, tn), lambda i, j: (i, nj + j)),
                  pl.BlockSpec((tk, tn), lambda i, j: (nk + i, nj + j)),
                  pl.BlockSpec((1, tn), lambda i, j: (0, j)),
                  pl.BlockSpec((1, tn), lambda i, j: (0, nj + j))],
        out_specs=[pl.BlockSpec((tk, tn), lambda i, j: (i, j)), pl.BlockSpec((tk, tn), lambda i, j: (i, j))],
        out_shape=[jax.ShapeDtypeStruct((ln, half), F32)] * 2, name="hy_filter_assemble",
        compiler_params=pltpu.CompilerParams(dimension_semantics=("arbitrary", "arbitrary")),
    )(spec, spec, spec, spec, ss, ss)


def _dft_fwd_kernel(fc_ref, fs_ref, z_ref, gc_ref, gs_ref, yc_ref, ys_ref, zb_ref, *, nb):
    @pl.when(pl.program_id(2) == 0)
    def _():
        zb_ref[...] = z_ref[...].astype(BF16)

    fc = fc_ref[...]
    fs = fs_ref[...]
    gcb = gc_ref[...]
    gsb = gs_ref[...]
    row_0 = (pl.program_id(2) == 0) & (lax.broadcasted_iota(jnp.int32, gcb.shape, 0) == 0)
    for s in range(nb):
        zc = jnp.dot(fc, zb_ref[s], preferred_element_type=F32)
        zs = jnp.dot(fs, zb_ref[s], preferred_element_type=F32)
        yc_ref[s] = jnp.where(row_0, zc * gcb, zc * gcb - zs * gsb).astype(yc_ref.dtype)
        ys_ref[s] = jnp.where(row_0, zs * gsb, zc * gsb + zs * gcb).astype(ys_ref.dtype)


def _dft_inv_kernel(ic_ref, is_ref, yc_ref, ys_ref, zin_ref, gate_ref, bias_ref, o_ref, *, nb):
    ic = ic_ref[...]
    isn = is_ref[...]
    bias = bias_ref[...]
    for s in range(nb):
        acc = jnp.dot(ic, yc_ref[s], preferred_element_type=F32)
        acc = acc + jnp.dot(isn, ys_ref[s], preferred_element_type=F32)
        o_ref[s] = (gate_ref[s] * (acc + zin_ref[s] * bias)).astype(o_ref.dtype)


def _long_conv_gated(z, z_col, pc, gate_col, gc, gs, order, bias, tables, *, ln, nseq, row0, out_dtype):
    fwd, inv_c, inv_s = tables
    d = D_MODEL
    tn = 512
    nj = d // tn
    tk = _pick(ln, (512, 256))
    nk = ln // tk
    nb = max(n for n in (8, 4, 2, 1) if nseq % n == 0 and n * ln <= 2048)
    z3 = z.reshape(z.shape[0] // ln, ln, z.shape[1])
    pc3 = pc.reshape(pc.shape[0] // ln, ln, pc.shape[1])
    z_sb0 = (row0 // ln if z.shape[0] != nseq * ln else 0) // nb
    pc_sb0 = (row0 // ln) // nb
    seq_blk = (nb, ln, tn)
    out_blk = (nb, tk, tn)
    cparams = pltpu.CompilerParams(
        dimension_semantics=("arbitrary",) * 3,
        vmem_limit_bytes=_vmem_limit(_nbytes(seq_blk, F32) * 2, _nbytes(out_blk, F32) * 3, _nbytes((tk, ln), BF16) * 2))
    yc, ys = pl.pallas_call(
        functools.partial(_dft_fwd_kernel, nb=nb), grid=(nseq // nb, nj, nk),
        in_specs=[pl.BlockSpec((tk, ln), lambda b, j, i: (i, 0)),
                  pl.BlockSpec((tk, ln), lambda b, j, i: (nk + i, 0)),
                  pl.BlockSpec(seq_blk, lambda b, j, i: (z_sb0 + b, 0, z_col * nj + j)),
                  pl.BlockSpec((tk, tn), lambda b, j, i: (i, order * nj + j)),
                  pl.BlockSpec((tk, tn), lambda b, j, i: (i, order * nj + j))],
        out_specs=[pl.BlockSpec(out_blk, lambda b, j, i: (b, i, j))] * 2,
        out_shape=[jax.ShapeDtypeStruct((nseq, ln, d), BF16)] * 2,
        scratch_shapes=[pltpu.VMEM(seq_blk, BF16)], name="hy_dft_fwd", compiler_params=cparams,
    )(fwd, fwd, z3, gc, gs)
    out = pl.pallas_call(
        functools.partial(_dft_inv_kernel, nb=nb), grid=(nseq // nb, nj, nk),
        in_specs=[pl.BlockSpec((tk, ln), lambda b, j, i: (i, 0)),
                  pl.BlockSpec((tk, ln), lambda b, j, i: (i, 0)),
                  pl.BlockSpec(seq_blk, lambda b, j, i: (b, 0, j)),
                  pl.BlockSpec(seq_blk, lambda b, j, i: (b, 0, j)),
                  pl.BlockSpec(out_blk, lambda b, j, i: (z_sb0 + b, i, z_col * nj + j)),
                  pl.BlockSpec(out_blk, lambda b, j, i: (pc_sb0 + b, i, gate_col * nj + j)),
                  pl.BlockSpec((None, 1, tn), lambda b, j, i: (order, 0, j))],
        out_specs=pl.BlockSpec(out_blk, lambda b, j, i: (b, i, j)),
        out_shape=jax.ShapeDtypeStruct((nseq, ln, d), out_dtype),
        name="hy_dft_inv", compiler_params=cparams,
    )(inv_c, inv_s, yc, ys, z3, pc3, bias[:, None, :])
    return out.reshape(nseq * ln, d)


def _hyena_in_conv(u, w_in, short, geo):
    assert short.shape[0] == 3
    rb = geo.l_s
    assert rb % geo.l_p == 0 and geo.l_p & (geo.l_p - 1) == 0 and rb & (rb - 1) == 0
    n_prompt_tiles = _tp(geo) // rb

    def conv3(accs, ex):
        acc, w = accs[0], ex[0]
        seq = jnp.where(pl.program_id(1) < n_prompt_tiles, geo.l_p, geo.l_s)
        pos = lax.broadcasted_iota(jnp.int32, acc.shape, 0) & (seq - 1)
        prev = jnp.where(pos == 0, 0.0, pltpu.roll(acc, 1, 0))
        nxt = jnp.where(pos == seq - 1, 0.0, pltpu.roll(acc, rb - 1, 0))
        return (w[0:1] * prev + w[1:2] * acc + w[2:3] * nxt,)

    return _linear(u, w_in, out_dtype=F32, tm=rb, epilogue=conv3, extras=[short],
                   extra_specs_fn=lambda i_of, j_of, tm, tn: [pl.BlockSpec((3, tn), lambda *g: (0, j_of(*g)))],
                   name="hy_in_conv")


def _hyena_mixer(u, geo, w_in, short, w1, b1, w2, b2, w3, filt_bias, w_out, res_extras):
    d = D_MODEL
    pc = _hyena_in_conv(u, w_in, short, geo)
    parts = []
    for ln, nseq, row0 in ((geo.l_p, geo.n_p, 0), (geo.l_s, geo.n_s, _tp(geo))):
        if nseq == 0:
            continue
        tables = _dft_tables(ln)
        gc, gs = _hyena_filters(ln, w1, b1, w2, b2, w3, tables[0])
        z1 = _long_conv_gated(pc, 0, pc, 1, gc, gs, 0, filt_bias, tables, ln=ln, nseq=nseq, row0=row0,
                              out_dtype=F32)
        z2 = _long_conv_gated(z1, 0, pc, 2, gc, gs, 1, filt_bias, tables, ln=ln, nseq=nseq, row0=row0,
                              out_dtype=BF16)
        parts.append(z2)
    z = jnp.concatenate(parts, axis=0) if len(parts) > 1 else parts[0]
    return _linear(z, w_out, out_dtype=F32, epilogue=_ep_residual, extras=res_extras,
                   extra_specs_fn=_residual_specs(geo), geo=geo, name="hy_out")


def _head_linear(a, ws, scales):
    t = a.shape[0]
    tm = _pick(t, (1024, 512, 256))
    nw = len(ws)
    return _mm(
        [a], [w.reshape(ML_HEADS * ML_DH, ML_DH) for w in ws], [(0, n) for n in range(nw)],
        grid=(ML_HEADS, t // tm),
        a_specs=[pl.BlockSpec((tm, ML_DH), lambda h, i: (i, h))],
        b_specs=[pl.BlockSpec((ML_DH, ML_DH), lambda h, i: (h, 0))] * nw,
        out_shapes=[jax.ShapeDtypeStruct((t, ML_HEADS * ML_DH), BF16)] * nw,
        out_specs=[pl.BlockSpec((tm, ML_DH), lambda h, i: (i, h))] * nw,
        epilogue=lambda accs, ex: tuple(acc if s is None else acc * s for acc, s in zip(accs, scales)),
        name="ml_head_linear")


def _gate_prep_kernel(pre_ref, o_ref):
    pre = pre_ref[...]
    lane = lax.broadcasted_iota(jnp.int32, pre.shape, 1)
    is_f = (lane % 16) >= ML_HEADS
    lf = jnp.minimum(pre, 0.0) - jnp.log(1.0 + jnp.exp(-jnp.abs(pre)))
    r = lax.broadcasted_iota(jnp.int32, (CHUNK, CHUNK), 0)
    c = lax.broadcasted_iota(jnp.int32, (CHUNK, CHUNK), 1)
    lower = (c <= r).astype(F32)
    upper = (c >= r).astype(F32)
    pre_sum = jnp.dot(lower, lf, preferred_element_type=F32, precision=HIGHEST)
    suf_sum = jnp.dot(upper, lf, preferred_element_type=F32, precision=HIGHEST)
    cum = jnp.where(lane < 16, pre_sum, suf_sum)
    o_ref[...] = jnp.where(is_f, cum, pre)


def _mlstm_scan_kernel(*refs, has_init, nc):
    if has_init:
        q_ref, k_ref, v_ref, gc_ref, gr_ref, c0_ref, n0_ref, m0_ref, h_ref, c_ref, n_ref, m_ref = refs
    else:
        q_ref, k_ref, v_ref, gc_ref, gr_ref, h_ref, c_ref, n_ref, m_ref = refs
    dr = pl.program_id(0)
    ci = pl.program_id(2)

    carry = has_init or nc > 1
    if carry:
        @pl.when(ci == 0)
        def _():
            if has_init:
                c_ref[...] = c0_ref[...]
                n_ref[...] = n0_ref[...]
                m_ref[...] = m0_ref[...]
            else:
                c_ref[...] = jnp.zeros_like(c_ref)
                n_ref[...] = jnp.zeros_like(n_ref)
                m_ref[...] = jnp.zeros_like(m_ref)

    gcv = gc_ref[...]
    lane = lax.broadcasted_iota(jnp.int32, gcv.shape, 1)
    rowi = lax.broadcasted_iota(jnp.int32, (CHUNK, CHUNK), 0)
    coli = lax.broadcasted_iota(jnp.int32, (CHUNK, CHUNK), 1)
    mask = (rowi - coli) * (1 - 2 * dr) >= 0
    for hh in range(ML_HEADS):
        cols = slice(hh * ML_DH, (hh + 1) * ML_DH)
        q = q_ref[:, cols]
        k = k_ref[:, cols]
        v = v_ref[:, cols]
        li_idx = dr * 16 + hh
        b_idx = li_idx + ML_HEADS
        bcol = jnp.sum(jnp.where(lane == b_idx, gcv, 0.0), axis=1, keepdims=True)
        licol = jnp.sum(jnp.where(lane == li_idx, gcv, 0.0), axis=1, keepdims=True)
        brow = gr_ref[pl.ds(b_idx, 1), :]
        lirow = gr_ref[pl.ds(li_idx, 1), :]
        m_prev = m_ref[hh] if carry else jnp.zeros((1, 1), F32)

        dmat = jnp.where(mask, bcol - brow + lirow, -jnp.inf)
        a = bcol + m_prev
        mt = jnp.maximum(a, jnp.max(dmat, axis=1, keepdims=True))
        qk = lax.dot_general(q, k, (((1,), (1,)), ((), ())), preferred_element_type=F32)
        s = qk * jnp.exp(dmat - mt)
        num = jnp.dot(s.astype(BF16), v, preferred_element_type=F32)
        den = jnp.sum(s, axis=1, keepdims=True)
        if carry:
            w_inter = jnp.exp(a - mt)
            c_prev = c_ref[hh]
            n_prev = n_ref[hh]
            num = num + w_inter * jnp.dot(q, c_prev.astype(BF16), preferred_element_type=F32)
            den = den + w_inter * jnp.sum(q.astype(F32) * n_prev, axis=1, keepdims=True)
        h_ref[:, cols] = num / jnp.maximum(jnp.abs(den), jnp.exp(-mt))

        b_last = jnp.min(bcol, axis=0, keepdims=True)
        dl = b_last - bcol + licol
        m_new = jnp.maximum(b_last + m_prev, jnp.max(dl, axis=0, keepdims=True))
        kw = k.astype(F32) * jnp.exp(dl - m_new)
        c_new = lax.dot_general(kw.astype(BF16), v, (((0,), (0,)), ((), ())), preferred_element_type=F32)
        n_new = jnp.sum(kw, axis=0, keepdims=True)
        if carry:
            w_old = jnp.exp(b_last + m_prev - m_new)
            c_new = w_old * c_prev + c_new
            n_new = w_old * n_prev + n_new
        c_ref[hh] = c_new
        n_ref[hh] = n_new
        m_ref[hh] = m_new


def _mlstm_scan(q, k, v, gcols, grows, *, ln, nseq, row0, init=None):
    nc = ln // CHUNK
    rb0 = row0 // CHUNK
    hd = ML_HEADS * ML_DH

    def blk(dr, b, c):
        return rb0 + b * nc + c + dr * (nc - 1 - 2 * c)

    row_spec = pl.BlockSpec((CHUNK, hd), lambda dr, b, c: (blk(dr, b, c), 0))
    in_specs = [row_spec, row_spec, row_spec,
                pl.BlockSpec((CHUNK, LANES), lambda dr, b, c: (blk(dr, b, c), 0)),
                pl.BlockSpec((LANES, CHUNK), lambda dr, b, c: (0, blk(dr, b, c)))]
    args = [q, k, v, gcols, grows]
    st_c = pl.BlockSpec((None, None, ML_HEADS, ML_DH, ML_DH), lambda dr, b, c: (b, dr, 0, 0, 0))
    st_n = pl.BlockSpec((None, None, ML_HEADS, 1, ML_DH), lambda dr, b, c: (b, dr, 0, 0, 0))
    st_m = pl.BlockSpec((None, None, ML_HEADS, 1, 1), lambda dr, b, c: (b, dr, 0, 0, 0))
    if init is not None:
        in_specs += [st_c, st_n, st_m]
        args += list(init)
    state_bytes = _nbytes((ML_HEADS, ML_DH, ML_DH), F32)
    return pl.pallas_call(
        functools.partial(_mlstm_scan_kernel, has_init=init is not None, nc=nc),
        grid=(2, nseq, nc), in_specs=in_specs,
        out_specs=[pl.BlockSpec((None, CHUNK, hd),
                                lambda dr, b, c: (dr, b * nc + c + dr * (nc - 1 - 2 * c), 0)),
                   st_c, st_n, st_m],
        out_shape=[jax.ShapeDtypeStruct((2, nseq * ln, hd), F32),
                   jax.ShapeDtypeStruct((nseq, 2, ML_HEADS, ML_DH, ML_DH), F32),
                   jax.ShapeDtypeStruct((nseq, 2, ML_HEADS, 1, ML_DH), F32),
                   jax.ShapeDtypeStruct((nseq, 2, ML_HEADS, 1, 1), F32)],
        name="mlstm_scan_l%d" % ln,
        compiler_params=pltpu.CompilerParams(
            dimension_semantics=("arbitrary",) * 3,
            vmem_limit_bytes=_vmem_limit(3 * _nbytes((CHUNK, hd), BF16), _nbytes((CHUNK, hd), F32),
                                         2 * state_bytes)),
    )(*args)


def _mlstm_post_fn(hf, hb, xc, z, norm_g, skip):
    h = hf + hb
    outs = []
    for hd in range(ML_HEADS):
        sl = slice(hd * ML_DH, (hd + 1) * ML_DH)
        hh = h[:, sl]
        mu = jnp.mean(hh, axis=-1, keepdims=True)
        var = jnp.mean(jnp.square(hh - mu), axis=-1, keepdims=True)
        outs.append((hh - mu) * lax.rsqrt(var + EPS))
    hn = jnp.concatenate(outs, axis=-1)
    y = hn * norm_g + skip * xc
    return (y * _silu(z),)


def _mlstm_mixer(u, geo, state, w_up, conv_w, w_q, w_k, w_v, w_gate, b_gate, norm_g, skip, w_down,
                 res_extras):
    inner = ML_HEADS * ML_DH
    t = u.shape[0]
    up = _linear(u, w_up, out_dtype=F32, name="ml_up")
    xc = _dwconv(up, conv_w, geo, n_ch=inner, post=_silu)
    q, k = _head_linear(xc, [w_q, w_k], [ML_DH ** -0.5, None])
    (v,) = _head_linear(up, [w_v], [None])
    wg = jnp.transpose(w_gate, (1, 0, 2)).reshape(3 * inner, 4 * ML_HEADS)
    wg = jnp.zeros((3 * inner, LANES), F32).at[:, :4 * ML_HEADS].set(wg)
    bg = jnp.zeros((1, LANES), F32).at[0, :4 * ML_HEADS].set(b_gate.reshape(-1))
    tm = _pick(t, (1024, 512, 256))
    (pre,) = _mm(
        [q, k, v], [wg, wg, wg], [(0, 0), (1, 1), (2, 2)], grid=(1, t // tm),
        a_specs=[pl.BlockSpec((tm, inner), lambda j, i: (i, 0))] * 3,
        b_specs=[pl.BlockSpec((inner, LANES), lambda j, i, r=r: (r, 0)) for r in range(3)],
        out_shapes=[jax.ShapeDtypeStruct((t, LANES), F32)],
        out_specs=[pl.BlockSpec((tm, LANES), lambda j, i: (i, 0))],
        epilogue=lambda accs, ex: (accs[0] + accs[1] + accs[2] + ex[0],),
        extras=[bg], extra_specs=[pl.BlockSpec((1, LANES), lambda j, i: (0, 0))], name="ml_gates")
    gcols = pl.pallas_call(
        _gate_prep_kernel, grid=(t // CHUNK,),
        in_specs=[pl.BlockSpec((CHUNK, LANES), lambda i: (i, 0))],
        out_specs=pl.BlockSpec((CHUNK, LANES), lambda i: (i, 0)),
        out_shape=jax.ShapeDtypeStruct((t, LANES), F32), name="mlstm_gate_prep",
        compiler_params=pltpu.CompilerParams(dimension_semantics=("arbitrary",)),
    )(pre)
    grows = gcols.T
    c0, n0, m0 = state
    hp, c_p, n_p, m_p = _mlstm_scan(q, k, v, gcols, grows, ln=geo.l_p, nseq=geo.n_p, row0=0)
    hs, _, _, _ = _mlstm_scan(q, k, v, gcols, grows, ln=geo.l_s, nseq=geo.n_s, row0=_tp(geo),
                              init=(c0, n0[:, :, :, None, :], m0[:, :, :, None, None]))
    ys = []
    for hdir, nrows, row0 in ((hp, _tp(geo), 0), (hs, t - _tp(geo), _tp(geo))):
        tmr = 256
        nb = nrows // tmr
        rb0 = row0 // tmr
        y = pl.pallas_call(
            functools.partial(_rowwise_kernel, nin=6, fn=_mlstm_post_fn),
            grid=(nb,),
            in_specs=[pl.BlockSpec((None, tmr, inner), lambda i: (0, i, 0)),
                      pl.BlockSpec((None, tmr, inner), lambda i: (1, i, 0)),
                      pl.BlockSpec((tmr, inner), lambda i: (rb0 + i, 0)),
                      pl.BlockSpec((tmr, inner), lambda i: (rb0 + i, 1)),
                      pl.BlockSpec((1, inner), lambda i: (0, 0)),
                      pl.BlockSpec((1, inner), lambda i: (0, 0))],
            out_specs=[pl.BlockSpec((tmr, inner), lambda i: (i, 0))],
            out_shape=[jax.ShapeDtypeStruct((nrows, inner), BF16)], name="mlstm_post",
            compiler_params=pltpu.CompilerParams(
                dimension_semantics=("arbitrary",),
                vmem_limit_bytes=_vmem_limit(5 * _nbytes((tmr, inner), F32))),
        )(hdir, hdir, xc, up, norm_g[None, :], skip[None, :])[0]
        ys.append(y)
    y = jnp.concatenate(ys, axis=0)
    x_new = _linear(y, w_down, out_dtype=F32, epilogue=_ep_residual, extras=res_extras,
                    extra_specs_fn=_residual_specs(geo), geo=geo, name="ml_down")
    return x_new, (c_p, n_p, m_p)


def _ln_silu_fn(y, g, b):
    mu = jnp.mean(y, axis=-1, keepdims=True)
    var = jnp.mean(jnp.square(y - mu), axis=-1, keepdims=True)
    return (_silu((y - mu) * lax.rsqrt(var + EPS) * g + b),)


def _conformer_mixer(u, geo, w1, b1, dw, b_dw, ln_g, ln_b, w2, b2, res_extras):
    d = D_MODEL
    b1r = b1[None, :]

    def glu(accs, ex):
        return ((accs[0] + ex[0]) * _sigmoid(accs[1] + ex[1]),)

    def especs(i_of, j_of, tm, tn):
        nj = d // tn
        return [pl.BlockSpec((1, tn), lambda *g: (0, j_of(*g))),
                pl.BlockSpec((1, tn), lambda *g: (0, nj + j_of(*g)))]

    y = _dual_linear(u, w1, w1, epilogue=glu, out_dtype=F32, n_out=d, w1_col=0, w2_col=1,
                     extras=[b1r, b1r], extra_specs_fn=especs, name="cv_glu")
    y = _dwconv(y, dw, geo, n_ch=d, bias=b_dw[None, :])
    (y,) = _rowwise(_ln_silu_fn, geo, rows=[(y, d, 0)], params=[(ln_g[None, :], d, 0), (ln_b[None, :], d, 0)],
                    outs=[(d, BF16)], tm=512, name="cv_ln_silu")
    return _linear(y, w2, out_dtype=F32, epilogue=_ep_residual_bias, extras=list(res_extras) + [b2[None, :]],
                   extra_specs_fn=_residual_bias_specs(geo), geo=geo, name="cv_out")


def _ep_swiglu(accs, ex):
    return (_silu(accs[0]) * accs[1],)


def _dense_ffn(u, geo, wg, wu, wd, layer, res_extras):
    _, d, f = wg.shape
    h = _dual_linear(u, wg.reshape(-1, f), wu.reshape(-1, f), epilogue=_ep_swiglu, out_dtype=BF16, n_out=f,
                     w_row=layer, name="ffn_up")
    return _linear(h, wd.reshape(-1, d), out_dtype=F32, epilogue=_ep_residual, extras=res_extras,
                   extra_specs_fn=_residual_specs(geo), geo=geo, k=f, w_row=layer, name="ffn_down")


def _lane_col(blk, idx):
    lane = lax.broadcasted_iota(jnp.int32, blk.shape, 1)
    return jnp.sum(jnp.where(lane == idx, blk, 0.0), axis=1, keepdims=True)


def _route_rank_kernel(info_ref, rank_ref, cnt_ref, carry_ref):
    @pl.when(pl.program_id(0) == 0)
    def _():
        carry_ref[...] = jnp.zeros_like(carry_ref)

    info = info_ref[...]
    lane = lax.broadcasted_iota(jnp.int32, info.shape, 1)
    lanef = lane.astype(F32)
    i1 = _lane_col(info, 0)
    i2 = _lane_col(info, 1)
    sel = jnp.where((lanef == i1) | (lanef == i2), 1.0, 0.0)
    r = lax.broadcasted_iota(jnp.int32, (CHUNK, CHUNK), 0)
    c = lax.broadcasted_iota(jnp.int32, (CHUNK, CHUNK), 1)
    strict = (c < r).astype(BF16)
    before = jnp.dot(strict, sel.astype(BF16), preferred_element_type=F32) + carry_ref[...]
    r1 = jnp.sum(jnp.where(lanef == i1, before, 0.0), axis=1, keepdims=True)
    r2 = jnp.sum(jnp.where(lanef == i2, before, 0.0), axis=1, keepdims=True)
    rank_ref[...] = jnp.where(lane == 0, r1, jnp.where(lane == 1, r2, 0.0))
    carry = carry_ref[...] + jnp.sum(sel, axis=0, keepdims=True)
    carry_ref[...] = carry
    cnt_ref[...] = carry


def _moe_up_kernel(te_ref, inv_ref, inv_next_ref, u_hbm, wg_ref, wu_ref, h_ref, buf, xs_b, sem, *, nt):
    i = pl.program_id(0)
    j = pl.program_id(1)
    n_used = te_ref[nt]
    _, tm, groups, _ = buf.shape

    def token_copy(src_tok, slot, dst_row):
        return pltpu.make_async_copy(u_hbm.at[src_tok], buf.at[slot, dst_row], sem.at[slot])

    def start_tile(idx_ref, slot):
        def body(r2, carry):
            token_copy(idx_ref[0, 2 * r2], slot, 2 * r2).start(priority=1)
            token_copy(idx_ref[0, 2 * r2 + 1], slot, 2 * r2 + 1).start(priority=1)
            return carry
        lax.fori_loop(0, tm // 2, body, 0, unroll=4)

    def finish_tile(slot):
        def body(r, carry):
            token_copy(0, slot, r).wait()
            return carry
        lax.fori_loop(0, tm, body, 0, unroll=8)
        xs_b[...] = jnp.concatenate([buf[slot, :, s, :] for s in range(groups)], axis=1).astype(BF16)

    @pl.when((j == 0) & (i < n_used))
    def _():
        @pl.when(i == 0)
        def _():
            start_tile(inv_ref, 0)

        for slot in range(2):
            @pl.when(i % 2 == slot)
            def _(slot=slot):
                @pl.when(i + 1 < n_used)
                def _():
                    start_tile(inv_next_ref, 1 - slot)

                finish_tile(slot)

    @pl.when(i < n_used)
    def _():
        a = xs_b[...]
        gate = jnp.dot(a, wg_ref[...].astype(BF16), preferred_element_type=F32)
        up = jnp.dot(a, wu_ref[...].astype(BF16), preferred_element_type=F32)
        h_ref[...] = (_silu(gate) * up).astype(h_ref.dtype)

    @pl.when(i >= n_used)
    def _():
        h_ref[...] = jnp.zeros_like(h_ref)


def _moe_down_kernel(te_ref, h_ref, w_ref, y_ref, w_b, *, nt):
    i = pl.program_id(1)

    @pl.when(i < te_ref[nt])
    def _():
        @pl.when((i == 0) | (te_ref[i] != te_ref[jnp.maximum(i - 1, 0)]))
        def _():
            w_b[...] = w_ref[...].astype(BF16)

        y_ref[...] = jnp.dot(h_ref[...], w_b[...], preferred_element_type=F32)

    @pl.when(i >= te_ref[nt])
    def _():
        y_ref[...] = jnp.zeros_like(y_ref)


def _moe_combine_kernel(p1_ref, p2_ref, ys_hbm, x_ref, g_ref, info_ref, o_ref, buf1, buf2, sem):
    tm = buf1.shape[0]

    def row_copy(src_row, buf, dst_row):
        return pltpu.make_async_copy(ys_hbm.at[pl.ds(src_row, 1), :], buf.at[pl.ds(dst_row, 1), :], sem)

    def issue(r, carry):
        row_copy(p1_ref[0, r], buf1, r).start(priority=0)
        row_copy(p2_ref[0, r], buf2, r).start(priority=1)
        return carry

    def drain(r, carry):
        row_copy(0, buf1, r).wait()
        row_copy(0, buf2, r).wait()
        return carry

    lax.fori_loop(0, tm, issue, 0, unroll=8)
    lax.fori_loop(0, tm, drain, 0, unroll=8)
    info = info_ref[...]
    y = _lane_col(info, 2) * buf1[...] + _lane_col(info, 3) * buf2[...]
    o_ref[...] = x_ref[...] + g_ref[...] * y


def _moe_ffn(u, info, geo, wg, wu, wd, layer, res_extras):
    t, groups, _ = u.shape
    d = groups * LANES
    _, ne, _, f = wg.shape
    tm = MOE_TM
    nt = (2 * t) // tm + ne
    rows = nt * tm
    rank, cnt = pl.pallas_call(
        _route_rank_kernel, grid=(t // CHUNK,),
        in_specs=[pl.BlockSpec((CHUNK, LANES), lambda i: (i, 0))],
        out_specs=[pl.BlockSpec((CHUNK, LANES), lambda i: (i, 0)), pl.BlockSpec((1, LANES), lambda i: (0, 0))],
        out_shape=[jax.ShapeDtypeStruct((t, LANES), F32), jax.ShapeDtypeStruct((1, LANES), F32)],
        scratch_shapes=[pltpu.VMEM((1, LANES), F32)], name="moe_rank",
        compiler_params=pltpu.CompilerParams(dimension_semantics=("arbitrary",)),
    )(info)
    counts = cnt[0, :ne].astype(jnp.int32)
    gsz = (counts + tm - 1) // tm * tm
    ends = jnp.cumsum(gsz)
    off = ends - gsz
    e1 = info[:, 0].astype(jnp.int32)
    e2 = info[:, 1].astype(jnp.int32)
    pos1 = off[e1] + rank[:, 0].astype(jnp.int32)
    pos2 = off[e2] + rank[:, 1].astype(jnp.int32)
    tok = jnp.arange(t, dtype=jnp.int32)
    inv = jnp.zeros((rows,), jnp.int32).at[jnp.concatenate([pos1, pos2])].set(jnp.concatenate([tok, tok]))
    tile_expert = jnp.sum(jnp.arange(nt, dtype=jnp.int32)[:, None] * tm >= ends[None, :], axis=1)
    te = jnp.concatenate([jnp.minimum(tile_expert, ne - 1), (ends[-1:] // tm)]).astype(jnp.int32)

    smem_row = lambda n: pl.BlockSpec((None, 1, n), lambda i, *_: (i, 0, 0), memory_space=pltpu.SMEM)

    def used(i, te_ref):
        return jnp.minimum(i, te_ref[nt] - 1)

    tn = _pick(f, (512, 256, 128))
    inv3 = inv.reshape(nt, 1, tm)
    h = pl.pallas_call(
        functools.partial(_moe_up_kernel, nt=nt),
        grid_spec=pltpu.PrefetchScalarGridSpec(
            num_scalar_prefetch=1, grid=(nt, f // tn),
            in_specs=[pl.BlockSpec((None, 1, tm), lambda i, j, te_ref: (i, 0, 0), memory_space=pltpu.SMEM),
                      pl.BlockSpec((None, 1, tm), lambda i, j, te_ref: (jnp.minimum(i + 1, nt - 1), 0, 0),
                                   memory_space=pltpu.SMEM),
                      pl.BlockSpec(memory_space=pl.ANY),
                      pl.BlockSpec((None, None, d, tn), lambda i, j, te_ref: (layer, te_ref[i], 0, j)),
                      pl.BlockSpec((None, None, d, tn), lambda i, j, te_ref: (layer, te_ref[i], 0, j))],
            out_specs=pl.BlockSpec((tm, tn), lambda i, j, te_ref: (i, j)),
            scratch_shapes=[pltpu.VMEM((2, tm, groups, LANES), F32), pltpu.VMEM((tm, d), BF16),
                            pltpu.SemaphoreType.DMA((2,))]),
        out_shape=jax.ShapeDtypeStruct((rows, f), BF16), name="moe_up",
        compiler_params=pltpu.CompilerParams(
            dimension_semantics=("arbitrary",) * 2,
            vmem_limit_bytes=_vmem_limit(_nbytes((tm, d), F32), _nbytes((d, tn), F32) * 3, _nbytes((tm, d), BF16),
                                         _nbytes((tm, tn), F32) * 3)),
    )(te, inv3, inv3, u, wg, wu)
    tno = 512
    ys = pl.pallas_call(
        functools.partial(_moe_down_kernel, nt=nt),
        grid_spec=pltpu.PrefetchScalarGridSpec(
            num_scalar_prefetch=1, grid=(d // tno, nt),
            in_specs=[pl.BlockSpec((tm, f), lambda j, i, te_ref: (used(i, te_ref), 0)),
                      pl.BlockSpec((None, None, f, tno), lambda j, i, te_ref: (layer, te_ref[i], 0, j))],
            out_specs=pl.BlockSpec((tm, tno), lambda j, i, te_ref: (i, j)),
            scratch_shapes=[pltpu.VMEM((f, tno), BF16)]),
        out_shape=jax.ShapeDtypeStruct((rows, d), F32), name="moe_down",
        compiler_params=pltpu.CompilerParams(
            dimension_semantics=("arbitrary",) * 2,
            vmem_limit_bytes=_vmem_limit(_nbytes((f, tno), F32), _nbytes((f, tno), BF16),
                                         _nbytes((tm, f), BF16), _nbytes((tm, tno), F32))),
    )(te, h, wd)

    x, gate = res_extras
    tc = CHUNK
    cond = _cond_of_block(geo, tc)
    return pl.pallas_call(
        _moe_combine_kernel, grid=(t // tc,),
        in_specs=[smem_row(tc), smem_row(tc), pl.BlockSpec(memory_space=pl.ANY),
                  pl.BlockSpec((tc, d), lambda i: (i, 0)),
                  pl.BlockSpec((None, 1, d), lambda i: (cond(i), 0, 0)),
                  pl.BlockSpec((tc, LANES), lambda i: (i, 0))],
        out_specs=pl.BlockSpec((tc, d), lambda i: (i, 0)),
        out_shape=jax.ShapeDtypeStruct((t, d), F32),
        scratch_shapes=[pltpu.VMEM((tc, d), F32), pltpu.VMEM((tc, d), F32), pltpu.SemaphoreType.DMA(())],
        name="moe_combine",
        compiler_params=pltpu.CompilerParams(dimension_semantics=("arbitrary",)),
    )(pos1.reshape(t // tc, 1, tc), pos2.reshape(t // tc, 1, tc), ys, x, gate, info)


def _trunk(x, conds, geo, state, p):
    depth = p["w_ada"].shape[0]
    d = D_MODEL
    nco = conds.shape[0]
    cpad = jnp.zeros((8, d), F32).at[:nco].set(conds)
    w_ada = p["w_ada"]
    tn = 512
    nj = 6 * d // tn
    (mod,) = _mm(
        [cpad], [w_ada], [(0, 0)], grid=(depth, nj, 1),
        a_specs=[pl.BlockSpec((8, d), lambda l, j, i: (0, 0))],
        b_specs=[pl.BlockSpec((None, d, tn), lambda l, j, i: (l, 0, j))],
        out_shapes=[jax.ShapeDtypeStruct((depth, 8, 6 * d), F32)],
        out_specs=[pl.BlockSpec((None, 8, tn), lambda l, j, i: (l, 0, j))],
        epilogue=lambda accs, ex: (accs[0] + ex[0],), a_fn=_silu,
        extras=[p["b_ada"][:, None, :]],
        extra_specs=[pl.BlockSpec((None, 1, tn), lambda l, j, i: (l, 0, j))], name="ada_mod")
    mod = mod[:, :nco].reshape(depth, nco, 6, 1, d)

    new_states = []
    for i in range(depth):
        sh1, sc1, g1, sh2, sc2, g2 = (mod[i, :, r] for r in range(6))
        u = _norm_mod(x, p["norm_mix_g"][i][None, :], sc1, sh1, geo)
        kind, j = i % 3, i // 3
        res = [x, g1]
        if kind == 0:
            x = _hyena_mixer(u, geo, p["hy_w_in"][j], p["hy_short"][j], p["hy_filt_w1"][j], p["hy_filt_b1"][j],
                             p["hy_filt_w2"][j], p["hy_filt_b2"][j], p["hy_filt_w3"][j], p["hy_filt_bias"][j],
                             p["hy_w_out"][j], res)
        elif kind == 1:
            st = tuple(s[:, j] for s in state)
            x, new_st = _mlstm_mixer(u, geo, st, p["ml_w_up"][j], p["ml_conv"][j], p["ml_w_q"][j],
                                     p["ml_w_k"][j], p["ml_w_v"][j], p["ml_w_gate"][j], p["ml_b_gate"][j],
                                     p["ml_norm_g"][j], p["ml_skip"][j], p["ml_w_down"][j], res)
            new_states.append(new_st)
        else:
            x = _conformer_mixer(u, geo, p["cv_w1"][j], p["cv_b1"][j], p["cv_dw"][j], p["cv_b_dw"][j],
                                 p["cv_ln_g"][j], p["cv_ln_b"][j], p["cv_w2"][j], p["cv_b2"][j], res)
        jf = i // 2
        g_ffn = p["norm_ffn_g"][i][None, :]
        if i % 2 == 0:
            u = _norm_mod(x, g_ffn, sc2, sh2, geo)
            x = _dense_ffn(u, geo, p["ff_w_gate"], p["ff_w_up"], p["ff_w_down"], jf, [x, g2])
        else:
            u, info = _norm_mod_route(x, g_ffn, sc2, sh2, p["moe_router"][jf], geo)
            x = _moe_ffn(u, info, geo, p["moe_w_gate"], p["moe_w_up"], p["moe_w_down"], jf, [x, g2])
    fg = p["final_norm_g"][None, :]
    outs = []
    for nrows, row0 in ((_tp(geo), 0), (_tt(geo) - _tp(geo), _tp(geo))):
        tm = 512
        (y,) = _rowwise(lambda xb, g: (_rms(xb, g),), geo, rows=[(x, d, 0)], params=[(fg, d, 0)],
                        outs=[(d, F32)], tm=tm, n_rows=nrows, row_block0=row0 // tm, name="final_norm")
        outs.append(y)
    new_state = tuple(jnp.stack(parts, axis=1) for parts in zip(*new_states))
    return outs[0], outs[1], new_state


def kernel(x_prompt, x_sample, state_mlstm_C, state_mlstm_n, state_mlstm_m, c, c_ctx, w_ada, b_ada, norm_mix_g, norm_ffn_g, hy_w_in, hy_short, hy_filt_w1, hy_filt_b1, hy_filt_w2, hy_filt_b2, hy_filt_w3, hy_filt_bias, hy_w_out, ml_w_up, ml_conv, ml_w_q, ml_w_k, ml_w_v, ml_w_gate, ml_b_gate, ml_norm_g, ml_skip, ml_w_down, cv_w1, cv_b1, cv_dw, cv_b_dw, cv_ln_g, cv_ln_b, cv_w2, cv_b2, ff_w_gate, ff_w_up, ff_w_down, moe_router, moe_w_gate, moe_w_up, moe_w_down, final_norm_g):
    p = dict(w_ada=w_ada, b_ada=b_ada, norm_mix_g=norm_mix_g, norm_ffn_g=norm_ffn_g, hy_w_in=hy_w_in,
             hy_short=hy_short, hy_filt_w1=hy_filt_w1, hy_filt_b1=hy_filt_b1, hy_filt_w2=hy_filt_w2,
             hy_filt_b2=hy_filt_b2, hy_filt_w3=hy_filt_w3, hy_filt_bias=hy_filt_bias, hy_w_out=hy_w_out,
             ml_w_up=ml_w_up, ml_conv=ml_conv, ml_w_q=ml_w_q, ml_w_k=ml_w_k, ml_w_v=ml_w_v,
             ml_w_gate=ml_w_gate, ml_b_gate=ml_b_gate, ml_norm_g=ml_norm_g, ml_skip=ml_skip,
             ml_w_down=ml_w_down, cv_w1=cv_w1, cv_b1=cv_b1, cv_dw=cv_dw, cv_b_dw=cv_b_dw, cv_ln_g=cv_ln_g,
             cv_ln_b=cv_ln_b, cv_w2=cv_w2, cv_b2=cv_b2, ff_w_gate=ff_w_gate, ff_w_up=ff_w_up,
             ff_w_down=ff_w_down, moe_router=moe_router, moe_w_gate=moe_w_gate, moe_w_up=moe_w_up,
             moe_w_down=moe_w_down, final_norm_g=final_norm_g)
    n_p, l_p, d = x_prompt.shape
    n_s, l_s, _ = x_sample.shape
    geo = Geo(n_p, l_p, n_s, l_s)
    x = jnp.concatenate([x_prompt.reshape(n_p * l_p, d), x_sample.reshape(n_s * l_s, d)], axis=0)
    conds = jnp.concatenate([c_ctx[None, :], c], axis=0)
    y_p, y_s, (c_new, n_new, m_new) = _trunk(
        x, conds, geo, (state_mlstm_C, state_mlstm_n, state_mlstm_m), p)
    n_ml = state_mlstm_C.shape[1]
    return (y_p.reshape(n_p, l_p, d), y_s.reshape(n_s, l_s, d),
            c_new.reshape(n_p, n_ml, 2, ML_HEADS, ML_DH, ML_DH),
            n_new.reshape(n_p, n_ml, 2, ML_HEADS, ML_DH),
            m_new.reshape(n_p, n_ml, 2, ML_HEADS))
```
